```python
import math
import jax, jax.numpy as jnp
from jax import lax
import numpy as np

D_MODEL = 4096
BATCH = 4
SEQ = 2048
DEPTH = 2
DEC_BATCH = 8
DEC_SEQ = 1
PAST_LEN = 16384
PAGE_SIZE = 128

N_A_LAYERS = DEPTH // 2
N_B_LAYERS = DEPTH - N_A_LAYERS
D_FF = ((8 * D_MODEL // 3 + 255) // 256) * 256
EXPAND = 2
D_INNER = EXPAND * D_MODEL
SSM_HEAD_DIM = 64
SSM_HEADS = D_INNER // SSM_HEAD_DIM
SSM_GROUPS = 8
HEADS_PER_GROUP = SSM_HEADS // SSM_GROUPS
D_STATE = 128
CONV_W = 4
GN = SSM_GROUPS * D_STATE
CONV_DIM = D_INNER + 2 * GN
D_IN_PROJ = 2 * D_INNER + 2 * GN + SSM_HEADS
CHUNK = 128
ATTN_HEAD_DIM = 128
ATTN_HEADS = D_MODEL // ATTN_HEAD_DIM
Q_BLOCK = 128
ATTN_SCALE = 1.0 / math.sqrt(ATTN_HEAD_DIM)
FORGET_BIAS_INIT = 3.0
EPS = 1e-6

kernel_name = "yoco_mamba2_fox_macaron_step"


def rms_norm(x, w):
    xf = x.astype(jnp.float32)
    y = xf * lax.rsqrt(jnp.mean(xf * xf, axis=-1, keepdims=True) + EPS)
    return (y * w.astype(jnp.float32)).astype(x.dtype)


def swiglu(x, w_in, w_out):
    g, u = jnp.split(x @ w_in, 2, axis=-1)
    return (jax.nn.silu(g) * u) @ w_out


def macaron_half(x, pre_w, post_w, w_in, w_out):
    return x + 0.5 * rms_norm(swiglu(rms_norm(x, pre_w), w_in, w_out), post_w)


def ssd_scan(X, dA, Bm, Cm, h0):
    b, L = X.shape[:2]
    cl = CHUNK if L % CHUNK == 0 else L
    nc = L // cl
    X = X.reshape(b, nc, cl, *X.shape[2:])
    dA = dA.reshape(b, nc, cl, *dA.shape[2:])
    Bm = Bm.reshape(b, nc, cl, *Bm.shape[2:])
    Cm = Cm.reshape(b, nc, cl, *Cm.shape[2:])
    a_cs = jnp.cumsum(dA, axis=2)
    causal = jnp.tril(jnp.ones((cl, cl), dtype=bool))
    seg = a_cs[:, :, :, None] - a_cs[:, :, None, :]
    decay_ls = jnp.exp(jnp.where(causal[:, :, None, None], seg, -jnp.inf))
    cb = jnp.einsum("bclgn,bcsgn->bclsg", Cm, Bm)
    y_diag = jnp.einsum("bclsg,bclsgr,bcsgrp->bclgrp", cb, decay_ls, X)
    decay_to_end = jnp.exp(a_cs[:, :, -1:] - a_cs)
    chunk_states = jnp.einsum("bclgn,bclgr,bclgrp->bcgrpn", Bm, decay_to_end, X)
    chunk_decay = jnp.exp(a_cs[:, :, -1])

    def carry_step(h, inp):
        s_c, d_c = inp
        return h * d_c[..., None, None] + s_c, h

    h_final, h_start = lax.scan(carry_step, h0,
                                (jnp.moveaxis(chunk_states, 1, 0), jnp.moveaxis(chunk_decay, 1, 0)))
    h_start = jnp.moveaxis(h_start, 0, 1)
    y_off = jnp.einsum("bclgn,bcgrpn,bclgr->bclgrp", Cm, h_start, jnp.exp(a_cs))
    return (y_diag + y_off).reshape(b, L, *X.shape[3:]), h_final


def mamba2_mixer(u, conv_buf, ssm_state, in_proj, conv_w, conv_b, dt_bias, a_log, d_skip, gnorm_w, out_proj):
    b, L, _ = u.shape
    zxbcdt = u @ in_proj
    z = zxbcdt[..., :D_INNER]
    xbc = zxbcdt[..., D_INNER:D_INNER + CONV_DIM]
    dt_raw = zxbcdt[..., D_INNER + CONV_DIM:]
    xpad = jnp.concatenate([conv_buf.astype(xbc.dtype), xbc], axis=1)
    new_conv = xpad[:, L:]
    acc = conv_b
    for tap in range(CONV_W):
        acc = acc + xpad[:, tap:tap + L] * conv_w[tap]
    xbc = jax.nn.silu(acc)
    xs = xbc[..., :D_INNER].reshape(b, L, SSM_GROUPS, HEADS_PER_GROUP, SSM_HEAD_DIM).astype(jnp.float32)
    bm = xbc[..., D_INNER:D_INNER + GN].reshape(b, L, SSM_GROUPS, D_STATE).astype(jnp.float32)
    cm = xbc[..., D_INNER + GN:].reshape(b, L, SSM_GROUPS, D_STATE).astype(jnp.float32)
    dt = jax.nn.softplus(dt_raw.astype(jnp.float32) + dt_bias.astype(jnp.float32))
    dt = dt.reshape(b, L, SSM_GROUPS, HEADS_PER_GROUP)
    a = -jnp.exp(a_log.astype(jnp.float32)).reshape(SSM_GROUPS, HEADS_PER_GROUP)
    h0 = ssm_state.astype(jnp.float32).reshape(b, SSM_GROUPS, HEADS_PER_GROUP, SSM_HEAD_DIM, D_STATE)
    y, h_final = ssd_scan(xs * dt[..., None], dt * a, bm, cm, h0)
    y = y + d_skip.astype(jnp.float32).reshape(SSM_GROUPS, HEADS_PER_GROUP)[:, :, None] * xs
    y = y.reshape(b, L, D_INNER) * jax.nn.silu(z.astype(jnp.float32))
    yg = y.reshape(b, L, SSM_GROUPS, D_INNER // SSM_GROUPS)
    yg = yg * lax.rsqrt(jnp.mean(yg * yg, axis=-1, keepdims=True) + EPS)
    y = (yg.reshape(b, L, D_INNER) * gnorm_w.astype(jnp.float32)).astype(u.dtype)
    new_state = h_final.reshape(b, SSM_HEADS, SSM_HEAD_DIM, D_STATE).astype(ssm_state.dtype)
    return y @ out_proj, new_conv, new_state


def shared_kv(h, kv_norm_w, w_kvf, b_fg):
    b, L, _ = h.shape
    proj = rms_norm(h, kv_norm_w) @ w_kvf
    hd = ATTN_HEADS * ATTN_HEAD_DIM
    k = proj[..., :hd].reshape(b, L, ATTN_HEADS, ATTN_HEAD_DIM)
    v = proj[..., hd:2 * hd].reshape(b, L, ATTN_HEADS, ATTN_HEAD_DIM)
    logf = jax.nn.log_sigmoid(proj[..., 2 * hd:].astype(jnp.float32) + b_fg.astype(jnp.float32))
    return k, v, logf


def fox_prompt(q, k, v, logf):
    b, L = q.shape[:2]
    nb = L // Q_BLOCK
    c = jnp.cumsum(logf, axis=1)
    c_k = jnp.moveaxis(c, 2, 1)
    qb = jnp.moveaxis(q.reshape(b, nb, Q_BLOCK, ATTN_HEADS, ATTN_HEAD_DIM), 1, 0)
    cqb = jnp.moveaxis(c.reshape(b, nb, Q_BLOCK, ATTN_HEADS), 1, 0)
    pos_k = jnp.arange(L)

    def block(args):
        i, qi, ci = args
        s = jnp.einsum("bqhd,bkhd->bhqk", qi, k).astype(jnp.float32) * ATTN_SCALE
        s = s + jnp.moveaxis(ci, 2, 1)[..., None] - c_k[:, :, None, :]
        pos_q = i * Q_BLOCK + jnp.arange(Q_BLOCK)
        s = jnp.where(pos_k[None, :] <= pos_q[:, None], s, -jnp.inf)
        p = jax.nn.softmax(s, axis=-1)
        return jnp.einsum("bhqk,bkhd->bqhd", p.astype(v.dtype), v)

    out = lax.map(block, (jnp.arange(nb), qb, cqb))
    return jnp.moveaxis(out, 0, 1).reshape(b, L, ATTN_HEADS * ATTN_HEAD_DIM)


def fox_sample(q, k_new, v_new, logf_new, cache_k, cache_v, cache_logf, page_table):
    b, L = q.shape[:2]
    past = page_table.shape[1] * PAGE_SIZE
    k_all = jnp.concatenate(
        [cache_k[page_table].reshape(b, past, ATTN_HEADS, ATTN_HEAD_DIM).astype(k_new.dtype), k_new], axis=1)
    v_all = jnp.concatenate(
        [cache_v[page_table].reshape(b, past, ATTN_HEADS, ATTN_HEAD_DIM).astype(v_new.dtype), v_new], axis=1)
    logf_all = jnp.concatenate(
        [cache_logf[page_table].reshape(b, past, ATTN_HEADS).astype(jnp.float32), logf_new], axis=1)
    c = jnp.cumsum(logf_all, axis=1)
    c_k = jnp.moveaxis(c, 2, 1)
    c_q = c_k[:, :, past:]
    s = jnp.einsum("bqhd,bkhd->bhqk", q, k_all).astype(jnp.float32) * ATTN_SCALE
    s = s + c_q[..., None] - c_k[:, :, None, :]
    mask = jnp.arange(past + L)[None, :] <= (past + jnp.arange(L))[:, None]
    s = jnp.where(mask, s, -jnp.inf)
    p = jax.nn.softmax(s, axis=-1)
    out = jnp.einsum("bhqk,bkhd->bqhd", p.astype(v_all.dtype), v_all)
    return out.reshape(b, L, ATTN_HEADS * ATTN_HEAD_DIM)


def setup_inputs(seed: int = 0) -> dict:
    key = jax.random.key(seed)
    ks = jax.random.split(key, 24)
    f32 = jnp.float32

    def nrm(k, shape, scale):
        return jax.random.normal(k, shape, f32) * scale

    n_pages = PAST_LEN // PAGE_SIZE
    n_used = DEC_BATCH * n_pages
    n_phys = n_used + (n_used + 3) // 4
    page_table = jax.random.permutation(ks[0], n_phys)[:n_used].reshape(DEC_BATCH, n_pages).astype(jnp.int32)
    hd = ATTN_HEADS * ATTN_HEAD_DIM
    dt0 = jnp.exp(jax.random.uniform(ks[1], (N_A_LAYERS, SSM_HEADS), f32, math.log(1e-3), math.log(1e-1)))
    return {
        "x_prompt": nrm(ks[2], (BATCH, SEQ, D_MODEL), 1.0),
        "x_sample": nrm(ks[3], (DEC_BATCH, DEC_SEQ, D_MODEL), 1.0),
        "state_conv": nrm(ks[4], (N_A_LAYERS, DEC_BATCH, CONV_W - 1, CONV_DIM), 1.0),
        "state_ssm": nrm(ks[5], (N_A_LAYERS, DEC_BATCH, SSM_HEADS, SSM_HEAD_DIM, D_STATE), 0.1),
        "cache_k": nrm(ks[6], (n_phys, PAGE_SIZE, ATTN_HEADS, ATTN_HEAD_DIM), 1.0),
        "cache_v": nrm(ks[7], (n_phys, PAGE_SIZE, ATTN_HEADS, ATTN_HEAD_DIM), 1.0),
        "cache_logf": jax.nn.log_sigmoid(FORGET_BIAS_INIT + nrm(ks[8], (n_phys, PAGE_SIZE, ATTN_HEADS), 1.0)),
        "page_table": page_table,
        "norm_w": 1.0 + nrm(ks[9], (DEPTH, 6, D_MODEL), 0.05),
        "ffn_w_in": nrm(ks[10], (DEPTH, 2, D_MODEL, 2 * D_FF), D_MODEL ** -0.5),
        "ffn_w_out": nrm(ks[11], (DEPTH, 2, D_FF, D_MODEL), D_FF ** -0.5),
        "m_in_proj": nrm(ks[12], (N_A_LAYERS, D_MODEL, D_IN_PROJ), D_MODEL ** -0.5),
        "m_conv_w": nrm(ks[13], (N_A_LAYERS, CONV_W, CONV_DIM), CONV_W ** -0.5),
        "m_conv_b": nrm(ks[14], (N_A_LAYERS, CONV_DIM), 0.02),
        "m_dt_bias": dt0 + jnp.log(-jnp.expm1(-dt0)),
        "m_A_log": jnp.log(jax.random.uniform(ks[15], (N_A_LAYERS, SSM_HEADS), f32, 1.0, 16.0)),
        "m_D": 1.0 + nrm(ks[16], (N_A_LAYERS, SSM_HEADS), 0.05),
        "m_gnorm_w": 1.0 + nrm(ks[17], (N_A_LAYERS, D_INNER), 0.05),
        "m_out_proj": nrm(ks[18], (N_A_LAYERS, D_INNER, D_MODEL), D_INNER ** -0.5),
        "kv_norm_w": 1.0 + nrm(ks[19], (D_MODEL,), 0.05),
        "w_kvf": nrm(ks[20], (D_MODEL, 2 * hd + ATTN_HEADS), D_MODEL ** -0.5),
        "b_fg": FORGET_BIAS_INIT + nrm(ks[21], (ATTN_HEADS,), 0.5),
        "w_q": nrm(ks[22], (N_B_LAYERS, D_MODEL, hd), D_MODEL ** -0.5),
        "w_o": nrm(ks[23], (N_B_LAYERS, hd, D_MODEL), hd ** -0.5),
    }


def reference(x_prompt, x_sample, state_conv, state_ssm, cache_k, cache_v, cache_logf, page_table,
              norm_w, ffn_w_in, ffn_w_out, m_in_proj, m_conv_w, m_conv_b, m_dt_bias, m_A_log, m_D,
              m_gnorm_w, m_out_proj, kv_norm_w, w_kvf, b_fg, w_q, w_o):
    y_p, y_s = x_prompt, x_sample
    bp = x_prompt.shape[0]
    zero_conv = jnp.zeros((bp, CONV_W - 1, CONV_DIM), x_prompt.dtype)
    zero_ssm = jnp.zeros((bp, SSM_HEADS, SSM_HEAD_DIM, D_STATE), x_prompt.dtype)
    conv_p, ssm_p, conv_s, ssm_s = [], [], [], []
    k_p = v_p = logf_p = k_s = v_s = logf_s = None
    for layer in range(DEPTH):
        nw = norm_w[layer]
        f_in, f_out = ffn_w_in[layer], ffn_w_out[layer]
        y_p = macaron_half(y_p, nw[0], nw[1], f_in[0], f_out[0])
        y_s = macaron_half(y_s, nw[0], nw[1], f_in[0], f_out[0])
        if layer < N_A_LAYERS:
            a = layer
            mp = (m_in_proj[a], m_conv_w[a], m_conv_b[a], m_dt_bias[a], m_A_log[a], m_D[a],
                  m_gnorm_w[a], m_out_proj[a])
            o_p, c_p, h_p = mamba2_mixer(rms_norm(y_p, nw[2]), zero_conv, zero_ssm, *mp)
            o_s, c_s, h_s = mamba2_mixer(rms_norm(y_s, nw[2]), state_conv[a], state_ssm[a], *mp)
            conv_p.append(c_p)
            ssm_p.append(h_p)
            conv_s.append(c_s)
            ssm_s.append(h_s)
        else:
            bl = layer - N_A_LAYERS
            q_p = (rms_norm(y_p, nw[2]) @ w_q[bl]).reshape(y_p.shape[0], y_p.shape[1], ATTN_HEADS, ATTN_HEAD_DIM)
            q_s = (rms_norm(y_s, nw[2]) @ w_q[bl]).reshape(y_s.shape[0], y_s.shape[1], ATTN_HEADS, ATTN_HEAD_DIM)
            o_p = fox_prompt(q_p, k_p, v_p, logf_p) @ w_o[bl]
            o_s = fox_sample(q_s, k_s, v_s, logf_s, cache_k, cache_v, cache_logf, page_table) @ w_o[bl]
        y_p = y_p + rms_norm(o_p, nw[3])
        y_s = y_s + rms_norm(o_s, nw[3])
        y_p = macaron_half(y_p, nw[4], nw[5], f_in[1], f_out[1])
        y_s = macaron_half(y_s, nw[4], nw[5], f_in[1], f_out[1])
        if layer == N_A_LAYERS - 1:
            k_p, v_p, logf_p = shared_kv(y_p, kv_norm_w, w_kvf, b_fg)
            k_s, v_s, logf_s = shared_kv(y_s, kv_norm_w, w_kvf, b_fg)
    return (y_p, y_s, jnp.stack(conv_p), jnp.stack(ssm_p), k_p, v_p, logf_p,
            jnp.stack(conv_s), jnp.stack(ssm_s), k_s, v_s, logf_s)
```

```python
import functools

import jax
import jax.numpy as jnp
from jax import lax
from jax.experimental import pallas as pl
from jax.experimental.pallas import tpu as pltpu

F32 = jnp.float32
BF16 = jnp.bfloat16
HIGHEST = lax.Precision.HIGHEST
EPS = 1e-6

SSM_HEAD_DIM = 64
SSM_GROUPS = 8
D_STATE = 128
CONV_W = 4
CHUNK = 128
ATTN_HEAD_DIM = 128
PAGE_SIZE = 128

LANES = 128
SAMPLE_ROWS = 16
VMEM_LIMIT = 56 * 1024 * 1024

NT_DIMS = (((1,), (1,)), ((), ()))


def _cparams(*semantics):
    return pltpu.CompilerParams(dimension_semantics=semantics, vmem_limit_bytes=VMEM_LIMIT)


def _rms(x, w):
    return x * lax.rsqrt(jnp.mean(x * x, axis=-1, keepdims=True) + EPS) * w


def _silu(x):
    return x * jax.nn.sigmoid(x)


def _softplus(x):
    return jnp.maximum(x, 0.0) + jnp.log1p(jnp.exp(-jnp.abs(x)))


def _dot(a, b):
    return jnp.dot(a, b, preferred_element_type=F32)


def _dot_exact(a, b):
    return jnp.dot(a, b, precision=HIGHEST, preferred_element_type=F32)


def _iota(shape, axis):
    return lax.broadcasted_iota(jnp.int32, shape, axis)


def _col_bcast(row):
    return jnp.broadcast_to(row, (LANES, LANES)).T


def _ffn_kernel(x_ref, pre_ref, post_ref, wg_ref, wu_ref, wo_ref, o_ref, xn_ref):
    f = pl.program_id(1)

    @pl.when(f == 0)
    def _():
        xn_ref[...] = _rms(x_ref[...], pre_ref[...]).astype(BF16)
        o_ref[...] = jnp.zeros_like(o_ref)

    xn = xn_ref[...]
    g = _dot(xn, wg_ref[...])
    u = _dot(xn, wu_ref[...])
    h = (_silu(g) * u).astype(BF16)
    o_ref[...] += _dot(h, wo_ref[...])

    @pl.when(f == pl.num_programs(1) - 1)
    def _():
        o_ref[...] = x_ref[...] + 0.5 * _rms(o_ref[...], post_ref[...])


def ffn_half(x, pre_w, post_w, w_in, w_out, *, tm, tf):
    m, d = x.shape
    dff = w_out.shape[0]
    nf = dff // tf
    return pl.pallas_call(
        _ffn_kernel,
        grid=(m // tm, nf),
        in_specs=[
            pl.BlockSpec((tm, d), lambda i, f: (i, 0), pipeline_mode=pl.Buffered(1)),
            pl.BlockSpec((1, d), lambda i, f: (0, 0)),
            pl.BlockSpec((1, d), lambda i, f: (0, 0)),
            pl.BlockSpec((d, tf), lambda i, f: (0, f)),
            pl.BlockSpec((d, tf), lambda i, f: (0, f + nf)),
            pl.BlockSpec((tf, d), lambda i, f: (f, 0)),
        ],
        out_specs=pl.BlockSpec((tm, d), lambda i, f: (i, 0)),
        out_shape=jax.ShapeDtypeStruct((m, d), F32),
        scratch_shapes=[pltpu.VMEM((tm, d), BF16)],
        compiler_params=_cparams("parallel", "arbitrary"),
        name="ffn_half",
    )(x, pre_w.reshape(1, d), post_w.reshape(1, d), w_in, w_in, w_out)


def _norm_matmul_kernel(*refs, has_bias, n_out):
    x_ref, nw_ref, w_ref = refs[:3]
    b_ref = refs[3] if has_bias else None
    o_refs = refs[3 + has_bias:3 + has_bias + n_out]
    xn_ref = refs[-1]

    @pl.when(pl.program_id(1) == 0)
    def _():
        xn_ref[...] = _rms(x_ref[...], nw_ref[...]).astype(BF16)

    acc = _dot(xn_ref[...], w_ref[...])
    if has_bias:
        acc = -_softplus(-(acc + b_ref[...]))
    for o_ref in o_refs:
        o_ref[...] = acc.astype(o_ref.dtype)


def norm_matmul(x, norm_w, w, *, tm, tn, out_dtypes=(F32,), logsig_bias=None):
    m, d = x.shape
    n = w.shape[1]
    has_bias = logsig_bias is not None
    in_specs = [
        pl.BlockSpec((tm, d), lambda i, j: (i, 0)),
        pl.BlockSpec((1, d), lambda i, j: (0, 0)),
        pl.BlockSpec((d, tn), lambda i, j: (0, j)),
    ]
    args = [x, norm_w.reshape(1, d), w]
    if has_bias:
        in_specs.append(pl.BlockSpec((1, tn), lambda i, j: (0, j)))
        args.append(logsig_bias.reshape(1, n))
    outs = pl.pallas_call(
        functools.partial(_norm_matmul_kernel, has_bias=has_bias, n_out=len(out_dtypes)),
        grid=(m // tm, n // tn),
        in_specs=in_specs,
        out_specs=[pl.BlockSpec((tm, tn), lambda i, j: (i, j)) for _ in out_dtypes],
        out_shape=[jax.ShapeDtypeStruct((m, n), dt) for dt in out_dtypes],
        scratch_shapes=[pltpu.VMEM((tm, d), BF16)],
        compiler_params=_cparams("parallel", "arbitrary"),
        name="norm_matmul",
    )(*args)
    return outs[0] if len(out_dtypes) == 1 else outs


def _matmul_norm_res_kernel(a_ref, w_ref, post_ref, res_ref, o_ref):
    k = pl.program_id(1)

    @pl.when(k == 0)
    def _():
        o_ref[...] = jnp.zeros_like(o_ref)

    o_ref[...] += _dot(a_ref[...], w_ref[...])

    @pl.when(k == pl.num_programs(1) - 1)
    def _():
        o_ref[...] = res_ref[...] + _rms(o_ref[...], post_ref[...])


def matmul_norm_residual(a, w, post_w, res, *, tm, tk):
    m, kdim = a.shape
    n = w.shape[1]
    return pl.pallas_call(
        _matmul_norm_res_kernel,
        grid=(m // tm, kdim // tk),
        in_specs=[
            pl.BlockSpec((tm, tk), lambda i, k: (i, k)),
            pl.BlockSpec((tk, n), lambda i, k: (k, 0)),
            pl.BlockSpec((1, n), lambda i, k: (0, 0)),
            pl.BlockSpec((tm, n), lambda i, k: (i, 0)),
        ],
        out_specs=pl.BlockSpec((tm, n), lambda i, k: (i, 0)),
        out_shape=jax.ShapeDtypeStruct((m, n), F32),
        compiler_params=_cparams("parallel", "arbitrary"),
        name="matmul_norm_residual",
    )(a, w, post_w.reshape(1, n), res)


def _conv_prompt_kernel(x_ref, w_ref, b_ref, o_ref, tail_ref):
    x = x_ref[0]
    seq = x.shape[0]
    row = _iota(x.shape, 0)
    acc = b_ref[...]
    for tap in range(CONV_W):
        back = CONV_W - 1 - tap
        xs = x if back == 0 else jnp.where(row >= back, pltpu.roll(x, back, 0), 0.0)
        acc = acc + xs * w_ref[tap:tap + 1, :]
    o_ref[0] = _silu(acc)
    tail_ref[0] = x[seq - (CONV_W - 1):, :]


def conv_prompt(xbc, conv_w, conv_b, *, batch, tc):
    t, c = xbc.shape
    seq = t // batch
    x3 = xbc.reshape(batch, seq, c)
    act, tail = pl.pallas_call(
        _conv_prompt_kernel,
        grid=(batch, c // tc),
        in_specs=[
            pl.BlockSpec((1, seq, tc), lambda b, j: (b, 0, j)),
            pl.BlockSpec((CONV_W, tc), lambda b, j: (0, j)),
            pl.BlockSpec((1, tc), lambda b, j: (0, j)),
        ],
        out_specs=[
            pl.BlockSpec((1, seq, tc), lambda b, j: (b, 0, j)),
            pl.BlockSpec((1, CONV_W - 1, tc), lambda b, j: (b, 0, j)),
        ],
        out_shape=[
            jax.ShapeDtypeStruct((batch, seq, c), F32),
            jax.ShapeDtypeStruct((batch, CONV_W - 1, c), F32),
        ],
        compiler_params=_cparams("parallel", "parallel"),
        name="conv_prompt",
    )(x3, conv_w, conv_b.reshape(1, c))
    return act.reshape(t, c), tail


def _ssd_prep_kernel(dtr_ref, bias_ref, alog_ref, dt_ref, acs_ref):
    dt = _softplus(dtr_ref[...] + bias_ref[...])
    da = dt * (-jnp.exp(alog_ref[...]))
    lower = (_iota((CHUNK, CHUNK), 1) <= _iota((CHUNK, CHUNK), 0)).astype(F32)
    dt_ref[...] = dt
    acs_ref[...] = _dot_exact(lower, da)


def ssd_prep(dt_raw, dt_bias, a_log):
    t, nh = dt_raw.shape
    blk = pl.BlockSpec((CHUNK, nh), lambda c: (c, 0))
    vec = pl.BlockSpec((1, nh), lambda c: (0, 0))
    return pl.pallas_call(
        _ssd_prep_kernel,
        grid=(t // CHUNK,),
        in_specs=[blk, vec, vec],
        out_specs=[blk, blk],
        out_shape=[jax.ShapeDtypeStruct((t, nh), F32)] * 2,
        compiler_params=_cparams("parallel"),
        name="ssd_prep",
    )(dt_raw, dt_bias.reshape(1, nh), a_log.reshape(1, nh))


def _ssd_kernel(x_ref, b_ref, c_ref, z_ref, dt_ref, acs_ref, acsc_ref, acst_ref, d_ref, gw_ref,
                y_ref, hfin_ref, ht_ref, yd_ref, *, heads_per_group):
    g = pl.program_id(1)
    c = pl.program_id(2)
    r_n, p_n = heads_per_group, SSM_HEAD_DIM
    gp = r_n * p_n
    nh = dt_ref.shape[1]

    @pl.when(c == 0)
    def _():
        ht_ref[...] = jnp.zeros_like(ht_ref)

    x = x_ref[...]
    bm = b_ref[...]
    cm_bf = c_ref[...].astype(BF16)
    expand = (_iota((nh, gp), 1) // p_n + g * r_n == _iota((nh, gp), 0)).astype(F32)
    dt_e = _dot_exact(dt_ref[...], expand)
    acs_e = _dot_exact(acs_ref[...], expand)
    xd = x * dt_e
    xd_bf = xd.astype(BF16)

    cb = lax.dot_general(cm_bf, bm.astype(BF16), NT_DIMS, preferred_element_type=F32)
    causal = _iota((CHUNK, CHUNK), 1) <= _iota((CHUNK, CHUNK), 0)
    acsc = acsc_ref[0, 0]
    acst = acst_ref[0, 0]
    for r in range(r_n):
        seg = jnp.broadcast_to(acsc[:, r:r + 1], (CHUNK, CHUNK)) - jnp.broadcast_to(acst[r:r + 1, :], (CHUNK, CHUNK))
        decay = jnp.exp(jnp.where(causal, seg, -jnp.inf))
        yd_ref[:, r * p_n:(r + 1) * p_n] = _dot((cb * decay).astype(BF16), xd_bf[:, r * p_n:(r + 1) * p_n])

    ht = ht_ref[...]
    y_off = _dot(cm_bf, ht.astype(BF16)) * jnp.exp(acs_e)
    last = acs_e[CHUNK - 1:CHUNK, :]
    xw = (xd * jnp.exp(last - acs_e)).astype(BF16)
    ht_new = ht * jnp.exp(last) + _dot(bm.T.astype(BF16), xw)
    ht_ref[...] = ht_new

    y = yd_ref[...] + y_off + d_ref[...] * x
    y = y * _silu(z_ref[...])
    y_ref[...] = _rms(y, gw_ref[...]).astype(y_ref.dtype)

    @pl.when(c == pl.num_programs(2) - 1)
    def _():
        hfin_ref[0] = ht_new.T


def ssd_prompt(xbc_act, z, dt, acs, d_exp, gnorm_w, *, batch):
    t, conv_dim = xbc_act.shape
    d_inner = z.shape[1]
    nh = dt.shape[1]
    groups = SSM_GROUPS
    r_n = nh // groups
    gp = d_inner // groups
    nc = t // batch // CHUNK
    n_chunks = t // CHUNK
    acs_g = acs.reshape(n_chunks, CHUNK, groups, r_n).transpose(0, 2, 1, 3)
    acs_gt = acs_g.transpose(0, 1, 3, 2)
    xoff = d_inner // D_STATE
    goff = groups * D_STATE // D_STATE
    row = lambda b, g, c: b * nc + c
    y, hfin = pl.pallas_call(
        functools.partial(_ssd_kernel, heads_per_group=r_n),
        grid=(batch, groups, nc),
        in_specs=[
            pl.BlockSpec((CHUNK, gp), lambda b, g, c: (row(b, g, c), g)),
            pl.BlockSpec((CHUNK, D_STATE), lambda b, g, c: (row(b, g, c), xoff + g)),
            pl.BlockSpec((CHUNK, D_STATE), lambda b, g, c: (row(b, g, c), xoff + goff + g)),
            pl.BlockSpec((CHUNK, gp), lambda b, g, c: (row(b, g, c), g)),
            pl.BlockSpec((CHUNK, nh), lambda b, g, c: (row(b, g, c), 0)),
            pl.BlockSpec((CHUNK, nh), lambda b, g, c: (row(b, g, c), 0)),
            pl.BlockSpec((1, 1, CHUNK, r_n), lambda b, g, c: (row(b, g, c), g, 0, 0)),
            pl.BlockSpec((1, 1, r_n, CHUNK), lambda b, g, c: (row(b, g, c), g, 0, 0)),
            pl.BlockSpec((1, gp), lambda b, g, c: (0, g)),
            pl.BlockSpec((1, gp), lambda b, g, c: (0, g)),
        ],
        out_specs=[
            pl.BlockSpec((CHUNK, gp), lambda b, g, c: (row(b, g, c), g)),
            pl.BlockSpec((1, gp, D_STATE), lambda b, g, c: (b, g, 0)),
        ],
        out_shape=[
            jax.ShapeDtypeStruct((t, d_inner), BF16),
            jax.ShapeDtypeStruct((batch, d_inner, D_STATE), F32),
        ],
        scratch_shapes=[pltpu.VMEM((D_STATE, gp), F32), pltpu.VMEM((CHUNK, gp), F32)],
        compiler_params=_cparams("parallel", "parallel", "arbitrary"),
        name="ssd_prompt",
    )(xbc_act, xbc_act, xbc_act, z, dt, acs, acs_g, acs_gt, d_exp.reshape(1, d_inner),
      gnorm_w.reshape(1, d_inner))
    return y, hfin


def _conv_step_kernel(st_ref, new_ref, w_ref, b_ref, act_ref, nst_ref):
    xn = new_ref[...]
    acc = b_ref[...]
    for tap in range(CONV_W - 1):
        acc = acc + st_ref[tap] * w_ref[tap:tap + 1, :]
        if tap > 0:
            nst_ref[tap - 1] = st_ref[tap]
    acc = acc + xn * w_ref[CONV_W - 1:CONV_W, :]
    nst_ref[CONV_W - 2] = xn
    act_ref[...] = _silu(acc)


def conv_step(state_t, xbc_new, conv_w, conv_b, *, tc):
    taps, batch, c = state_t.shape
    return pl.pallas_call(
        _conv_step_kernel,
        grid=(c // tc,),
        in_specs=[
            pl.BlockSpec((taps, batch, tc), lambda j: (0, 0, j)),
            pl.BlockSpec((batch, tc), lambda j: (0, j)),
            pl.BlockSpec((CONV_W, tc), lambda j: (0, j)),
            pl.BlockSpec((1, tc), lambda j: (0, j)),
        ],
        out_specs=[
            pl.BlockSpec((batch, tc), lambda j: (0, j)),
            pl.BlockSpec((taps, batch, tc), lambda j: (0, 0, j)),
        ],
        out_shape=[
            jax.ShapeDtypeStruct((batch, c), F32),
            jax.ShapeDtypeStruct((taps, batch, c), F32),
        ],
        compiler_params=_cparams("parallel"),
        name="conv_step",
    )(state_t, xbc_new, conv_w, conv_b.reshape(1, c))


def _ssd_step_kernel(h_ref, xs_ref, b_ref, c_ref, z_ref, dtr_ref, bias_ref, alog_ref, d_ref, gw_ref,
                     ho_ref, y_ref, dte_ref, dece_ref, *, heads_per_group):
    g = pl.program_id(0)
    r_n, p_n = heads_per_group, SSM_HEAD_DIM
    gp = r_n * p_n
    batch, nh = dtr_ref.shape
    rows_per_tile = LANES // p_n

    expand = (_iota((nh, gp), 1) // p_n + g * r_n == _iota((nh, gp), 0)).astype(F32)
    dt = _softplus(dtr_ref[...] + bias_ref[...])
    dte_ref[...] = _dot_exact(dt, expand)
    dece_ref[...] = _dot_exact(jnp.exp(dt * (-jnp.exp(alog_ref[...]))), expand)

    def body(b, carry):
        x = xs_ref[pl.ds(b, 1), :]
        bv = b_ref[pl.ds(b, 1), :]
        cv = c_ref[pl.ds(b, 1), :]
        xdt = x * dte_ref[pl.ds(b, 1), :]
        dec = dece_ref[pl.ds(b, 1), :]
        y_parts = []
        for j in range(gp // LANES):
            lanes = slice(j * LANES, (j + 1) * LANES)
            heads = pl.ds(j * rows_per_tile, rows_per_tile)
            h = h_ref[b, heads].reshape(LANES, D_STATE)
            h_new = h * _col_bcast(dec[:, lanes]) + _col_bcast(xdt[:, lanes]) * bv
            ho_ref[b, heads] = h_new.reshape(rows_per_tile, p_n, D_STATE)
            ycol = jnp.sum(h_new * cv, axis=-1, keepdims=True)
            y_parts.append(jnp.broadcast_to(ycol, (LANES, LANES)).T[0:1, :])
        y = jnp.concatenate(y_parts, axis=1) + d_ref[...] * x
        y = y * _silu(z_ref[pl.ds(b, 1), :])
        y_ref[pl.ds(b, 1), :] = _rms(y, gw_ref[...])
        return carry

    lax.fori_loop(0, batch, body, 0)


def ssd_step(h, xbc_act, z, dt_raw, dt_bias, a_log, d_exp, gnorm_w):
    batch, nh, p_n, n_n = h.shape
    d_inner = nh * p_n
    groups = SSM_GROUPS
    r_n = nh // groups
    gp = d_inner // groups
    xoff = d_inner // D_STATE
    return pl.pallas_call(
        functools.partial(_ssd_step_kernel, heads_per_group=r_n),
        grid=(groups,),
        in_specs=[
            pl.BlockSpec((batch, r_n, p_n, n_n), lambda g: (0, g, 0, 0)),
            pl.BlockSpec((batch, gp), lambda g: (0, g)),
            pl.BlockSpec((batch, D_STATE), lambda g: (0, xoff + g)),
            pl.BlockSpec((batch, D_STATE), lambda g: (0, xoff + groups + g)),
            pl.BlockSpec((batch, gp), lambda g: (0, g)),
            pl.BlockSpec((batch, nh), lambda g: (0, 0)),
            pl.BlockSpec((1, nh), lambda g: (0, 0)),
            pl.BlockSpec((1, nh), lambda g: (0, 0)),
            pl.BlockSpec((1, gp), lambda g: (0, g)),
            pl.BlockSpec((1, gp), lambda g: (0, g)),
        ],
        out_specs=[
            pl.BlockSpec((batch, r_n, p_n, n_n), lambda g: (0, g, 0, 0)),
            pl.BlockSpec((batch, gp), lambda g: (0, g)),
        ],
        out_shape=[
            jax.ShapeDtypeStruct(h.shape, F32),
            jax.ShapeDtypeStruct((batch, d_inner), F32),
        ],
        scratch_shapes=[pltpu.VMEM((batch, gp), F32), pltpu.VMEM((batch, gp), F32)],
        compiler_params=_cparams("parallel"),
        name="ssd_step",
    )(h, xbc_act, xbc_act, xbc_act, z, dt_raw, dt_bias.reshape(1, nh), a_log.reshape(1, nh),
      d_exp.reshape(1, d_inner), gnorm_w.reshape(1, d_inner))


def _cumsum_kernel(x_ref, o_ref, *, tile):
    seq = x_ref.shape[1]
    lower = (_iota((tile, tile), 1) <= _iota((tile, tile), 0)).astype(F32)
    carry = jnp.zeros((1, x_ref.shape[2]), F32)
    for i in range(seq // tile):
        cs = _dot_exact(lower, x_ref[0, i * tile:(i + 1) * tile, :]) + carry
        o_ref[0, i * tile:(i + 1) * tile, :] = cs
        carry = cs[tile - 1:tile, :]


def cumsum_seq(x3, *, tile=256):
    batch, seq, n = x3.shape
    blk = pl.BlockSpec((1, seq, n), lambda b: (b, 0, 0))
    return pl.pallas_call(
        functools.partial(_cumsum_kernel, tile=tile),
        grid=(batch,),
        in_specs=[blk],
        out_specs=blk,
        out_shape=jax.ShapeDtypeStruct(x3.shape, F32),
        compiler_params=_cparams("parallel"),
        name="cumsum_seq",
    )(x3)


def _fox_prompt_kernel(q_ref, k_ref, v_ref, c_ref, o_ref, *, tq, scale):
    qi = pl.program_id(2)
    q = q_ref[...]
    sub = tq // LANES
    cq_rows = c_ref[0, pl.ds(qi * sub, sub), :]
    cq = jnp.concatenate([_col_bcast(cq_rows[i:i + 1, :]) for i in range(sub)], axis=0)
    cq = jnp.concatenate([cq] * sub, axis=1)

    def block(ki, carry, masked):
        m, l, acc = carry
        k0 = pl.multiple_of(ki * tq, tq)
        k = k_ref[pl.ds(k0, tq), :]
        v = v_ref[pl.ds(k0, tq), :]
        ck_rows = c_ref[0, pl.ds(ki * sub, sub), :]
        ck = jnp.concatenate([jnp.broadcast_to(ck_rows[j:j + 1, :], (tq, LANES)) for j in range(sub)], axis=1)
        s = lax.dot_general(q, k, NT_DIMS, preferred_element_type=F32) * scale
        s = s + cq - ck
        if masked:
            s = jnp.where(_iota((tq, tq), 1) <= _iota((tq, tq), 0), s, -jnp.inf)
        m_new = jnp.maximum(m, jnp.max(s, axis=-1, keepdims=True))
        alpha = jnp.exp(m - m_new)
        p = jnp.exp(s - m_new)
        l = alpha * l + jnp.sum(p, axis=-1, keepdims=True)
        acc = alpha * acc + _dot(p.astype(BF16), v)
        return m_new, l, acc

    init = (jnp.full((tq, 1), -jnp.inf, F32), jnp.zeros((tq, 1), F32), jnp.zeros((tq, q.shape[1]), F32))
    carry = lax.fori_loop(0, qi, lambda ki, cr: block(ki, cr, False), init)
    _, l, acc = block(qi, carry, True)
    o_ref[...] = (acc / l).astype(o_ref.dtype)


def fox_prompt(q, k, v, c_tab, *, batch, heads, tq):
    t, hd = q.shape
    dh = hd // heads
    seq = t // batch
    nq = seq // tq
    return pl.pallas_call(
        functools.partial(_fox_prompt_kernel, tq=tq, scale=1.0 / (dh ** 0.5)),
        grid=(batch, heads, nq),
        in_specs=[
            pl.BlockSpec((tq, dh), lambda b, h, i: (b * nq + i, h)),
            pl.BlockSpec((seq, dh), lambda b, h, i: (b, h)),
            pl.BlockSpec((seq, dh), lambda b, h, i: (b, h)),
            pl.BlockSpec((1, seq // LANES, LANES), lambda b, h, i: (b * heads + h, 0, 0)),
        ],
        out_specs=pl.BlockSpec((tq, dh), lambda b, h, i: (b * nq + i, h)),
        out_shape=jax.ShapeDtypeStruct((t, hd), BF16),
        compiler_params=_cparams("parallel", "parallel", "arbitrary"),
        name="fox_prompt",
    )(q, k, v, c_tab)


def _fox_sample_kernel(pt_ref, q_ref, kn_ref, vn_ref, lfn_ref, ck_ref, cv_ref, clf_ref, o_ref,
                       qbd_ref, m_ref, l_ref, acc_ref, suf_ref, *, heads, scale):
    del pt_ref
    p = pl.program_id(1)
    hd = q_ref.shape[2]
    dh = hd // heads
    page = ck_ref.shape[1]
    own_head = _iota((heads, hd), 1) // dh == _iota((heads, hd), 0)

    @pl.when(p == 0)
    def _():
        qbd_ref[...] = jnp.where(own_head, jnp.broadcast_to(q_ref[0], (heads, hd)), 0.0).astype(BF16)
        m_ref[...] = jnp.full_like(m_ref, -jnp.inf)
        l_ref[...] = jnp.zeros_like(l_ref)
        acc_ref[...] = jnp.zeros_like(acc_ref)
        suf_ref[...] = lfn_ref[0]

    k = ck_ref[0].astype(BF16)
    v = cv_ref[0].astype(BF16)
    eye = (_iota((heads, heads), 0) == _iota((heads, heads), 1)).astype(F32)
    lf_t = lax.dot_general(eye, clf_ref[0], NT_DIMS, precision=HIGHEST, preferred_element_type=F32)
    later = (_iota((page, page), 0) > _iota((page, page), 1)).astype(F32)
    bias = _dot_exact(lf_t, later) + suf_ref[...]
    s = lax.dot_general(qbd_ref[...], k, NT_DIMS, preferred_element_type=F32) * scale + bias
    m_prev = m_ref[...]
    m_new = jnp.maximum(m_prev, jnp.max(s, axis=-1, keepdims=True))
    alpha = jnp.exp(m_prev - m_new)
    pr = jnp.exp(s - m_new)
    l_ref[...] = alpha * l_ref[...] + jnp.sum(pr, axis=-1, keepdims=True)
    acc_ref[...] = alpha * acc_ref[...] + _dot(pr.astype(BF16), v)
    m_ref[...] = m_new
    suf_ref[...] = suf_ref[...] + jnp.sum(lf_t, axis=-1, keepdims=True)

    @pl.when(p == pl.num_programs(1) - 1)
    def _():
        qf = qbd_ref[...].astype(F32)
        kn = kn_ref[0].astype(BF16).astype(F32)
        vn = vn_ref[0].astype(BF16).astype(F32)
        s_new = jnp.sum(qf * kn, axis=-1, keepdims=True) * scale
        m_fin = jnp.maximum(m_new, s_new)
        a_fin = jnp.exp(m_new - m_fin)
        p_new = jnp.exp(s_new - m_fin)
        l_fin = a_fin * l_ref[...] + p_new
        acc = a_fin * acc_ref[...] + p_new.astype(BF16).astype(F32) * vn
        out = jnp.where(own_head, acc / l_fin, 0.0)
        o_ref[0] = jnp.sum(out, axis=0, keepdims=True)


def fox_sample(q, k_new, v_new, logf_new, cache_k, cache_v, cache_logf, page_table, *, heads):
    batch, hd = q.shape
    n_phys, page = cache_k.shape[:2]
    n_pages = page_table.shape[1]
    dh = hd // heads
    row3 = lambda a: a.reshape(batch, 1, hd)
    new_spec = pl.BlockSpec((1, 1, hd), lambda b, p, pt: (b, 0, 0))
    phys = lambda b, p, pt: (pt[b * n_pages + (n_pages - 1 - p)], 0, 0)
    grid_spec = pltpu.PrefetchScalarGridSpec(
        num_scalar_prefetch=1,
        grid=(batch, n_pages),
        in_specs=[
            new_spec, new_spec, new_spec,
            pl.BlockSpec((1, heads, 1), lambda b, p, pt: (b, 0, 0)),
            pl.BlockSpec((1, page, hd), phys),
            pl.BlockSpec((1, page, hd), phys),
            pl.BlockSpec((1, page, heads), phys),
        ],
        out_specs=pl.BlockSpec((1, 1, hd), lambda b, p, pt: (b, 0, 0)),
        scratch_shapes=[
            pltpu.VMEM((heads, hd), BF16),
            pltpu.VMEM((heads, 1), F32),
            pltpu.VMEM((heads, 1), F32),
            pltpu.VMEM((heads, hd), F32),
            pltpu.VMEM((heads, 1), F32),
        ],
    )
    out = pl.pallas_call(
        functools.partial(_fox_sample_kernel, heads=heads, scale=1.0 / (dh ** 0.5)),
        grid_spec=grid_spec,
        out_shape=jax.ShapeDtypeStruct((batch, 1, hd), F32),
        compiler_params=_cparams("parallel", "arbitrary"),
        name="fox_sample",
    )(page_table.reshape(-1), row3(q), row3(k_new), row3(v_new), logf_new.reshape(batch, heads, 1),
      cache_k.reshape(n_phys, page, hd), cache_v.reshape(n_phys, page, hd), cache_logf)
    return out.reshape(batch, hd)


def _tiles(m):
    return min(m, 512)


def _trunk(y, w, *, batch, mamba_mixer, fox_mixer):
    m, d = y.shape
    tm = _tiles(m)
    nw = w["norm_w"]
    extras = {}
    for layer in range(2):
        y = ffn_half(y, nw[layer, 0], nw[layer, 1], w["ffn_in"][layer][0], w["ffn_out"][layer][0], tm=tm, tf=256)
        if layer == 0:
            z = norm_matmul(y, nw[0, 2], w["in_z"], tm=tm, tn=512)
            xbc = norm_matmul(y, nw[0, 2], w["in_xbc"], tm=tm, tn=512)
            dt_raw = norm_matmul(y, nw[0, 2], w["in_dt"], tm=tm, tn=LANES)
            mixed, extras["conv"], extras["ssm"] = mamba_mixer(z, xbc, dt_raw)
            y = matmul_norm_residual(mixed, w["out_proj"], nw[0, 3], y, tm=tm, tk=512)
        else:
            q = norm_matmul(y, nw[1, 2], w["w_q"], tm=tm, tn=512, out_dtypes=(BF16,))
            attn = fox_mixer(q, extras)
            y = matmul_norm_residual(attn, w["w_o"], nw[1, 3], y, tm=tm, tk=512)
        y = ffn_half(y, nw[layer, 4], nw[layer, 5], w["ffn_in"][layer][1], w["ffn_out"][layer][1], tm=tm, tf=256)
        if layer == 0:
            hd = w["w_k"].shape[1]
            extras["k"], extras["k_bf"] = norm_matmul(y, w["kv_norm_w"], w["w_k"], tm=tm, tn=512, out_dtypes=(F32, BF16))
            extras["v"], extras["v_bf"] = norm_matmul(y, w["kv_norm_w"], w["w_v"], tm=tm, tn=512, out_dtypes=(F32, BF16))
            extras["logf"] = norm_matmul(y, w["kv_norm_w"], w["w_f"], tm=tm, tn=LANES, logsig_bias=w["b_f"])
    return y, extras


def kernel(x_prompt, x_sample, state_conv, state_ssm, cache_k, cache_v, cache_logf, page_table,
           norm_w, ffn_w_in, ffn_w_out, m_in_proj, m_conv_w, m_conv_b, m_dt_bias, m_A_log, m_D,
           m_gnorm_w, m_out_proj, kv_norm_w, w_kvf, b_fg, w_q, w_o):
    bp, seq, d = x_prompt.shape
    bs = x_sample.shape[0]
    heads = cache_k.shape[2]
    dh = cache_k.shape[3]
    hd = heads * dh
    nh = m_dt_bias.shape[1]
    d_inner = nh * SSM_HEAD_DIM
    conv_dim = m_conv_w.shape[2]

    pad_f = LANES - heads
    w = {
        "norm_w": norm_w,
        "ffn_in": [[ffn_w_in[l, i].astype(BF16) for i in range(2)] for l in range(2)],
        "ffn_out": [[ffn_w_out[l, i].astype(BF16) for i in range(2)] for l in range(2)],
        "in_z": m_in_proj[0, :, :d_inner].astype(BF16),
        "in_xbc": m_in_proj[0, :, d_inner:d_inner + conv_dim].astype(BF16),
        "in_dt": m_in_proj[0, :, d_inner + conv_dim:].astype(BF16),
        "out_proj": m_out_proj[0].astype(BF16),
        "kv_norm_w": kv_norm_w,
        "w_k": w_kvf[:, :hd].astype(BF16),
        "w_v": w_kvf[:, hd:2 * hd].astype(BF16),
        "w_f": jnp.pad(w_kvf[:, 2 * hd:], ((0, 0), (0, pad_f))).astype(BF16),
        "b_f": jnp.pad(b_fg, (0, pad_f)),
        "w_q": w_q[0].astype(BF16),
        "w_o": w_o[0].astype(BF16),
    }
    d_exp = jnp.repeat(m_D[0], SSM_HEAD_DIM)
    conv_w, conv_b = m_conv_w[0], m_conv_b[0]

    def mamba_prompt(z, xbc, dt_raw):
        act, tail = conv_prompt(xbc, conv_w, conv_b, batch=bp, tc=512)
        dt, acs = ssd_prep(dt_raw, m_dt_bias[0], m_A_log[0])
        yb, hfin = ssd_prompt(act, z, dt, acs, d_exp, m_gnorm_w[0], batch=bp)
        return yb, tail, hfin.reshape(bp, nh, SSM_HEAD_DIM, D_STATE)

    def fox_prompt_mixer(q, ex):
        logf3 = ex["logf"].reshape(bp, seq, LANES)
        c = cumsum_seq(logf3)[:, :, :heads]
        c_tab = c.transpose(0, 2, 1).reshape(bp * heads, seq // LANES, LANES)
        return fox_prompt(q, ex["k_bf"], ex["v_bf"], c_tab, batch=bp, heads=heads, tq=512)

    y_p, ex_p = _trunk(x_prompt.reshape(bp * seq, d), w, batch=bp,
                       mamba_mixer=mamba_prompt, fox_mixer=fox_prompt_mixer)

    def mamba_sample(z, xbc, dt_raw):
        st = state_conv[0].transpose(1, 0, 2)
        act, nst = conv_step(st, xbc, conv_w, conv_b, tc=2048)
        h_new, yg = ssd_step(state_ssm[0], act, z, dt_raw, m_dt_bias[0], m_A_log[0], d_exp, m_gnorm_w[0])
        yg = jnp.pad(yg, ((0, SAMPLE_ROWS - bs), (0, 0))).astype(BF16)
        return yg, nst.transpose(1, 0, 2), h_new

    def fox_sample_mixer(q, ex):
        attn = fox_sample(q[:bs].astype(F32), ex["k"][:bs], ex["v"][:bs], ex["logf"][:bs, :heads],
                          cache_k, cache_v, cache_logf, page_table, heads=heads)
        return jnp.pad(attn, ((0, SAMPLE_ROWS - bs), (0, 0))).astype(BF16)

    xs_rows = jnp.pad(x_sample.reshape(bs, d), ((0, SAMPLE_ROWS - bs), (0, 0)))
    y_s, ex_s = _trunk(xs_rows, w, batch=bs, mamba_mixer=mamba_sample, fox_mixer=fox_sample_mixer)

    return (
        y_p.reshape(bp, seq, d),
        y_s[:bs].reshape(bs, 1, d),
        ex_p["conv"][None],
        ex_p["ssm"][None],
        ex_p["k"].reshape(bp, seq, heads, dh),
        ex_p["v"].reshape(bp, seq, heads, dh),
        ex_p["logf"][:, :heads].reshape(bp, seq, heads),
        ex_s["conv"][None],
        ex_s["ssm"][None],
        ex_s["k"][:bs].reshape(bs, 1, heads, dh),
        ex_s["v"][:bs].reshape(bs, 1, heads, dh),
        ex_s["logf"][:bs, :heads].reshape(bs, 1, heads),
    )
```

```python
import functools

import jax
import jax.numpy as jnp
from jax import lax
from jax.experimental import pallas as pl
from jax.experimental.pallas import tpu as pltpu

F32 = jnp.float32
BF16 = jnp.bfloat16
HIGHEST = lax.Precision.HIGHEST
EPS = 1e-6

SSM_HEAD_DIM = 64
SSM_GROUPS = 8
D_STATE = 128
CONV_W = 4
CHUNK = 128
ATTN_HEAD_DIM = 128
PAGE_SIZE = 128

LANES = 128
SAMPLE_ROWS = 16
PROJ_TN = 1024
VMEM_LIMIT = 56 * 1024 * 1024

NT_DIMS = (((1,), (1,)), ((), ()))


def _cparams(*semantics):
    return pltpu.CompilerParams(dimension_semantics=semantics, vmem_limit_bytes=VMEM_LIMIT)


def _rms(x, w):
    return x * lax.rsqrt(jnp.mean(x * x, axis=-1, keepdims=True) + EPS) * w


def _silu(x):
    return x * jax.nn.sigmoid(x)


def _softplus(x):
    return jnp.maximum(x, 0.0) + jnp.log1p(jnp.exp(-jnp.abs(x)))


def _dot(a, b):
    return jnp.dot(a, b, preferred_element_type=F32)


def _dot_exact(a, b):
    return jnp.dot(a, b, precision=HIGHEST, preferred_element_type=F32)


def _iota(shape, axis):
    return lax.broadcasted_iota(jnp.int32, shape, axis)


def _col_bcast(row):
    return jnp.broadcast_to(row, (LANES, LANES)).T


def _ffn_kernel(x_ref, pre_ref, post_ref, wg_ref, wu_ref, wo_ref, o_ref, xn_ref):
    f = pl.program_id(1)

    @pl.when(f == 0)
    def _():
        xn_ref[...] = _rms(x_ref[...], pre_ref[...]).astype(BF16)
        o_ref[...] = jnp.zeros_like(o_ref)

    xn = xn_ref[...]
    g = _dot(xn, wg_ref[...])
    u = _dot(xn, wu_ref[...])
    h = (_silu(g) * u).astype(BF16)
    o_ref[...] += _dot(h, wo_ref[...])

    @pl.when(f == pl.num_programs(1) - 1)
    def _():
        o_ref[...] = x_ref[...] + 0.5 * _rms(o_ref[...], post_ref[...])


def ffn_half(x, pre_w, post_w, w_in, w_out, *, tm, tf):
    m, d = x.shape
    dff = w_out.shape[0]
    nf = dff // tf
    return pl.pallas_call(
        _ffn_kernel,
        grid=(m // tm, nf),
        in_specs=[
            pl.BlockSpec((tm, d), lambda i, f: (i, 0), pipeline_mode=pl.Buffered(1)),
            pl.BlockSpec((1, d), lambda i, f: (0, 0)),
            pl.BlockSpec((1, d), lambda i, f: (0, 0)),
            pl.BlockSpec((d, tf), lambda i, f: (0, f)),
            pl.BlockSpec((d, tf), lambda i, f: (0, f + nf)),
            pl.BlockSpec((tf, d), lambda i, f: (f, 0)),
        ],
        out_specs=pl.BlockSpec((tm, d), lambda i, f: (i, 0)),
        out_shape=jax.ShapeDtypeStruct((m, d), F32),
        scratch_shapes=[pltpu.VMEM((tm, d), BF16)],
        compiler_params=_cparams("parallel", "arbitrary"),
        name="ffn_half",
    )(x, pre_w.reshape(1, d), post_w.reshape(1, d), w_in, w_in, w_out)


def _norm_matmul_kernel(*refs, has_bias, n_out):
    x_ref, nw_ref, w_ref = refs[:3]
    b_ref = refs[3] if has_bias else None
    o_refs = refs[3 + has_bias:3 + has_bias + n_out]
    xn_ref = refs[-1]

    @pl.when(pl.program_id(1) == 0)
    def _():
        xn_ref[...] = _rms(x_ref[...], nw_ref[...]).astype(BF16)

    acc = _dot(xn_ref[...], w_ref[...])
    if has_bias:
        acc = -_softplus(-(acc + b_ref[...]))
    for o_ref in o_refs:
        o_ref[...] = acc.astype(o_ref.dtype)


def norm_matmul(x, norm_w, w, *, tm, tn, cols=None, out_dtypes=(F32,), logsig_bias=None):
    m, d = x.shape
    col0, n = cols if cols is not None else (0, w.shape[1])
    tn = min(tn, n)
    first = col0 // tn
    assert first * tn == col0 and n % tn == 0 and m % tm == 0
    has_bias = logsig_bias is not None
    in_specs = [
        pl.BlockSpec((tm, d), lambda i, j: (i, 0), pipeline_mode=pl.Buffered(1)),
        pl.BlockSpec((1, d), lambda i, j: (0, 0)),
        pl.BlockSpec((d, tn), lambda i, j: (0, first + j)),
    ]
    args = [x, norm_w.reshape(1, d), w]
    if has_bias:
        in_specs.append(pl.BlockSpec((1, tn), lambda i, j: (0, j)))
        args.append(logsig_bias.reshape(1, n))
    outs = pl.pallas_call(
        functools.partial(_norm_matmul_kernel, has_bias=has_bias, n_out=len(out_dtypes)),
        grid=(m // tm, n // tn),
        in_specs=in_specs,
        out_specs=[pl.BlockSpec((tm, tn), lambda i, j: (i, j)) for _ in out_dtypes],
        out_shape=[jax.ShapeDtypeStruct((m, n), dt) for dt in out_dtypes],
        scratch_shapes=[pltpu.VMEM((tm, d), BF16)],
        compiler_params=_cparams("parallel", "arbitrary"),
        name="norm_matmul",
    )(*args)
    return outs[0] if len(out_dtypes) == 1 else outs


def _matmul_norm_res_kernel(a_ref, w_ref, post_ref, res_ref, o_ref):
    k = pl.program_id(1)

    @pl.when(k == 0)
    def _():
        o_ref[...] = jnp.zeros_like(o_ref)

    o_ref[...] += _dot(a_ref[...], w_ref[...])

    @pl.when(k == pl.num_programs(1) - 1)
    def _():
        o_ref[...] = res_ref[...] + _rms(o_ref[...], post_ref[...])


def matmul_norm_residual(a, w, post_w, res, *, tm, tk):
    m, kdim = a.shape
    n = w.shape[1]
    return pl.pallas_call(
        _matmul_norm_res_kernel,
        grid=(m // tm, kdim // tk),
        in_specs=[
            pl.BlockSpec((tm, tk), lambda i, k: (i, k)),
            pl.BlockSpec((tk, n), lambda i, k: (k, 0)),
            pl.BlockSpec((1, n), lambda i, k: (0, 0)),
            pl.BlockSpec((tm, n), lambda i, k: (i, 0)),
        ],
        out_specs=pl.BlockSpec((tm, n), lambda i, k: (i, 0)),
        out_shape=jax.ShapeDtypeStruct((m, n), F32),
        compiler_params=_cparams("parallel", "arbitrary"),
        name="matmul_norm_residual",
    )(a, w, post_w.reshape(1, n), res)


def _conv_prompt_kernel(x_ref, w_ref, b_ref, o_ref, tail_ref):
    x = x_ref[0]
    seq = x.shape[0]
    row = _iota(x.shape, 0)
    acc = b_ref[...]
    for tap in range(CONV_W):
        back = CONV_W - 1 - tap
        xs = x if back == 0 else jnp.where(row >= back, pltpu.roll(x, back, 0), 0.0)
        acc = acc + xs * w_ref[tap:tap + 1, :]
    o_ref[0] = _silu(acc)
    tail_ref[0] = x[seq - (CONV_W - 1):, :]


def conv_prompt(xbc, conv_w, conv_b, *, batch, tc):
    t, c = xbc.shape
    seq = t // batch
    x3 = xbc.reshape(batch, seq, c)
    act, tail = pl.pallas_call(
        _conv_prompt_kernel,
        grid=(batch, c // tc),
        in_specs=[
            pl.BlockSpec((1, seq, tc), lambda b, j: (b, 0, j)),
            pl.BlockSpec((CONV_W, tc), lambda b, j: (0, j)),
            pl.BlockSpec((1, tc), lambda b, j: (0, j)),
        ],
        out_specs=[
            pl.BlockSpec((1, seq, tc), lambda b, j: (b, 0, j)),
            pl.BlockSpec((1, CONV_W - 1, tc), lambda b, j: (b, 0, j)),
        ],
        out_shape=[
            jax.ShapeDtypeStruct((batch, seq, c), F32),
            jax.ShapeDtypeStruct((batch, CONV_W - 1, c), F32),
        ],
        compiler_params=_cparams("parallel", "parallel"),
        name="conv_prompt",
    )(x3, conv_w, conv_b.reshape(1, c))
    return act.reshape(t, c), tail


def _ssd_prep_kernel(dtr_ref, bias_ref, alog_ref, dt_ref, acs_ref):
    dt = _softplus(dtr_ref[...] + bias_ref[...])
    da = dt * (-jnp.exp(alog_ref[...]))
    lower = (_iota((CHUNK, CHUNK), 1) <= _iota((CHUNK, CHUNK), 0)).astype(F32)
    dt_ref[...] = dt
    acs_ref[...] = _dot_exact(lower, da)


def ssd_prep(dt_raw, dt_bias, a_log):
    t, nh = dt_raw.shape
    blk = pl.BlockSpec((CHUNK, nh), lambda c: (c, 0))
    vec = pl.BlockSpec((1, nh), lambda c: (0, 0))
    return pl.pallas_call(
        _ssd_prep_kernel,
        grid=(t // CHUNK,),
        in_specs=[blk, vec, vec],
        out_specs=[blk, blk],
        out_shape=[jax.ShapeDtypeStruct((t, nh), F32)] * 2,
        compiler_params=_cparams("parallel"),
        name="ssd_prep",
    )(dt_raw, dt_bias.reshape(1, nh), a_log.reshape(1, nh))


def _ssd_kernel(x_ref, b_ref, c_ref, z_ref, dt_ref, acs_ref, acsc_ref, acst_ref, d_ref, gw_ref,
                y_ref, hfin_ref, ht_ref, yd_ref, *, heads_per_group):
    g = pl.program_id(1)
    c = pl.program_id(2)
    r_n, p_n = heads_per_group, SSM_HEAD_DIM
    gp = r_n * p_n
    nh = dt_ref.shape[1]

    @pl.when(c == 0)
    def _():
        ht_ref[...] = jnp.zeros_like(ht_ref)

    x = x_ref[...]
    bm = b_ref[...]
    cm_bf = c_ref[...].astype(BF16)
    expand = (_iota((nh, gp), 1) // p_n + g * r_n == _iota((nh, gp), 0)).astype(F32)
    dt_e = _dot_exact(dt_ref[...], expand)
    acs_e = _dot_exact(acs_ref[...], expand)
    xd = x * dt_e
    xd_bf = xd.astype(BF16)

    cb = lax.dot_general(cm_bf, bm.astype(BF16), NT_DIMS, preferred_element_type=F32)
    causal = _iota((CHUNK, CHUNK), 1) <= _iota((CHUNK, CHUNK), 0)
    acsc = acsc_ref[0, 0]
    acst = acst_ref[0, 0]
    for r in range(r_n):
        seg = jnp.broadcast_to(acsc[:, r:r + 1], (CHUNK, CHUNK)) - jnp.broadcast_to(acst[r:r + 1, :], (CHUNK, CHUNK))
        decay = jnp.exp(jnp.where(causal, seg, -jnp.inf))
        yd_ref[:, r * p_n:(r + 1) * p_n] = _dot((cb * decay).astype(BF16), xd_bf[:, r * p_n:(r + 1) * p_n])

    ht = ht_ref[...]
    y_off = _dot(cm_bf, ht.astype(BF16)) * jnp.exp(acs_e)
    last = acs_e[CHUNK - 1:CHUNK, :]
    xw = (xd * jnp.exp(last - acs_e)).astype(BF16)
    ht_new = ht * jnp.exp(last) + _dot(bm.T.astype(BF16), xw)
    ht_ref[...] = ht_new

    y = yd_ref[...] + y_off + d_ref[...] * x
    y = y * _silu(z_ref[...])
    y_ref[...] = _rms(y, gw_ref[...]).astype(y_ref.dtype)

    @pl.when(c == pl.num_programs(2) - 1)
    def _():
        hfin_ref[0] = ht_new.T


def ssd_prompt(xbc_act, z, dt, acs, d_exp, gnorm_w, *, batch):
    t, conv_dim = xbc_act.shape
    d_inner = z.shape[1]
    nh = dt.shape[1]
    groups = SSM_GROUPS
    r_n = nh // groups
    gp = d_inner // groups
    nc = t // batch // CHUNK
    n_chunks = t // CHUNK
    acs_g = acs.reshape(n_chunks, CHUNK, groups, r_n).transpose(0, 2, 1, 3)
    acs_gt = acs_g.transpose(0, 1, 3, 2)
    xoff = d_inner // D_STATE
    goff = groups * D_STATE // D_STATE
    row = lambda b, g, c: b * nc + c
    y, hfin = pl.pallas_call(
        functools.partial(_ssd_kernel, heads_per_group=r_n),
        grid=(batch, groups, nc),
        in_specs=[
            pl.BlockSpec((CHUNK, gp), lambda b, g, c: (row(b, g, c), g)),
            pl.BlockSpec((CHUNK, D_STATE), lambda b, g, c: (row(b, g, c), xoff + g)),
            pl.BlockSpec((CHUNK, D_STATE), lambda b, g, c: (row(b, g, c), xoff + goff + g)),
            pl.BlockSpec((CHUNK, gp), lambda b, g, c: (row(b, g, c), g)),
            pl.BlockSpec((CHUNK, nh), lambda b, g, c: (row(b, g, c), 0)),
            pl.BlockSpec((CHUNK, nh), lambda b, g, c: (row(b, g, c), 0)),
            pl.BlockSpec((1, 1, CHUNK, r_n), lambda b, g, c: (row(b, g, c), g, 0, 0)),
            pl.BlockSpec((1, 1, r_n, CHUNK), lambda b, g, c: (row(b, g, c), g, 0, 0)),
            pl.BlockSpec((1, gp), lambda b, g, c: (0, g)),
            pl.BlockSpec((1, gp), lambda b, g, c: (0, g)),
        ],
        out_specs=[
            pl.BlockSpec((CHUNK, gp), lambda b, g, c: (row(b, g, c), g)),
            pl.BlockSpec((1, gp, D_STATE), lambda b, g, c: (b, g, 0)),
        ],
        out_shape=[
            jax.ShapeDtypeStruct((t, d_inner), BF16),
            jax.ShapeDtypeStruct((batch, d_inner, D_STATE), F32),
        ],
        scratch_shapes=[pltpu.VMEM((D_STATE, gp), F32), pltpu.VMEM((CHUNK, gp), F32)],
        compiler_params=_cparams("parallel", "parallel", "arbitrary"),
        name="ssd_prompt",
    )(xbc_act, xbc_act, xbc_act, z, dt, acs, acs_g, acs_gt, d_exp.reshape(1, d_inner),
      gnorm_w.reshape(1, d_inner))
    return y, hfin


def _conv_step_kernel(st_ref, new_ref, w_ref, b_ref, act_ref, nst_ref):
    xn = new_ref[...]
    acc = b_ref[...]
    for tap in range(CONV_W - 1):
        acc = acc + st_ref[tap] * w_ref[tap:tap + 1, :]
        if tap > 0:
            nst_ref[tap - 1] = st_ref[tap]
    acc = acc + xn * w_ref[CONV_W - 1:CONV_W, :]
    nst_ref[CONV_W - 2] = xn
    act_ref[...] = _silu(acc)


def conv_step(state_t, xbc_new, conv_w, conv_b, *, tc):
    taps, batch, c = state_t.shape
    return pl.pallas_call(
        _conv_step_kernel,
        grid=(c // tc,),
        in_specs=[
            pl.BlockSpec((taps, batch, tc), lambda j: (0, 0, j)),
            pl.BlockSpec((batch, tc), lambda j: (0, j)),
            pl.BlockSpec((CONV_W, tc), lambda j: (0, j)),
            pl.BlockSpec((1, tc), lambda j: (0, j)),
        ],
        out_specs=[
            pl.BlockSpec((batch, tc), lambda j: (0, j)),
            pl.BlockSpec((taps, batch, tc), lambda j: (0, 0, j)),
        ],
        out_shape=[
            jax.ShapeDtypeStruct((batch, c), F32),
            jax.ShapeDtypeStruct((taps, batch, c), F32),
        ],
        compiler_params=_cparams("parallel"),
        name="conv_step",
    )(state_t, xbc_new, conv_w, conv_b.reshape(1, c))


def _ssd_step_kernel(h_ref, xs_ref, b_ref, c_ref, z_ref, dtr_ref, bias_ref, alog_ref, d_ref, gw_ref,
                     ho_ref, y_ref, dte_ref, dece_ref, *, heads_per_group):
    g = pl.program_id(0)
    r_n, p_n = heads_per_group, SSM_HEAD_DIM
    gp = r_n * p_n
    batch, nh = dtr_ref.shape
    rows_per_tile = LANES // p_n

    expand = (_iota((nh, gp), 1) // p_n + g * r_n == _iota((nh, gp), 0)).astype(F32)
    dt = _softplus(dtr_ref[...] + bias_ref[...])
    dte_ref[...] = _dot_exact(dt, expand)
    dece_ref[...] = _dot_exact(jnp.exp(dt * (-jnp.exp(alog_ref[...]))), expand)

    def body(b, carry):
        x = xs_ref[pl.ds(b, 1), :]
        bv = b_ref[pl.ds(b, 1), :]
        cv = c_ref[pl.ds(b, 1), :]
        xdt = x * dte_ref[pl.ds(b, 1), :]
        dec = dece_ref[pl.ds(b, 1), :]
        y_parts = []
        for j in range(gp // LANES):
            lanes = slice(j * LANES, (j + 1) * LANES)
            heads = pl.ds(j * rows_per_tile, rows_per_tile)
            h = h_ref[b, heads].reshape(LANES, D_STATE)
            h_new = h * _col_bcast(dec[:, lanes]) + _col_bcast(xdt[:, lanes]) * bv
            ho_ref[b, heads] = h_new.reshape(rows_per_tile, p_n, D_STATE)
            ycol = jnp.sum(h_new * cv, axis=-1, keepdims=True)
            y_parts.append(jnp.broadcast_to(ycol, (LANES, LANES)).T[0:1, :])
        y = jnp.concatenate(y_parts, axis=1) + d_ref[...] * x
        y = y * _silu(z_ref[pl.ds(b, 1), :])
        y_ref[pl.ds(b, 1), :] = _rms(y, gw_ref[...])
        return carry

    lax.fori_loop(0, batch, body, 0)


def ssd_step(h, xbc_act, z, dt_raw, dt_bias, a_log, d_exp, gnorm_w):
    batch, nh, p_n, n_n = h.shape
    d_inner = nh * p_n
    groups = SSM_GROUPS
    r_n = nh // groups
    gp = d_inner // groups
    xoff = d_inner // D_STATE
    return pl.pallas_call(
        functools.partial(_ssd_step_kernel, heads_per_group=r_n),
        grid=(groups,),
        in_specs=[
            pl.BlockSpec((batch, r_n, p_n, n_n), lambda g: (0, g, 0, 0)),
            pl.BlockSpec((batch, gp), lambda g: (0, g)),
            pl.BlockSpec((batch, D_STATE), lambda g: (0, xoff + g)),
            pl.BlockSpec((batch, D_STATE), lambda g: (0, xoff + groups + g)),
            pl.BlockSpec((batch, gp), lambda g: (0, g)),
            pl.BlockSpec((batch, nh), lambda g: (0, 0)),
            pl.BlockSpec((1, nh), lambda g: (0, 0)),
            pl.BlockSpec((1, nh), lambda g: (0, 0)),
            pl.BlockSpec((1, gp), lambda g: (0, g)),
            pl.BlockSpec((1, gp), lambda g: (0, g)),
        ],
        out_specs=[
            pl.BlockSpec((batch, r_n, p_n, n_n), lambda g: (0, g, 0, 0)),
            pl.BlockSpec((batch, gp), lambda g: (0, g)),
        ],
        out_shape=[
            jax.ShapeDtypeStruct(h.shape, F32),
            jax.ShapeDtypeStruct((batch, d_inner), F32),
        ],
        scratch_shapes=[pltpu.VMEM((batch, gp), F32), pltpu.VMEM((batch, gp), F32)],
        compiler_params=_cparams("parallel"),
        name="ssd_step",
    )(h, xbc_act, xbc_act, xbc_act, z, dt_raw, dt_bias.reshape(1, nh), a_log.reshape(1, nh),
      d_exp.reshape(1, d_inner), gnorm_w.reshape(1, d_inner))


def _cumsum_kernel(x_ref, o_ref, *, tile):
    seq = x_ref.shape[1]
    lower = (_iota((tile, tile), 1) <= _iota((tile, tile), 0)).astype(F32)
    carry = jnp.zeros((1, x_ref.shape[2]), F32)
    for i in range(seq // tile):
        cs = _dot_exact(lower, x_ref[0, i * tile:(i + 1) * tile, :]) + carry
        o_ref[0, i * tile:(i + 1) * tile, :] = cs
        carry = cs[tile - 1:tile, :]


def cumsum_seq(x3, *, tile=256):
    batch, seq, n = x3.shape
    blk = pl.BlockSpec((1, seq, n), lambda b: (b, 0, 0))
    return pl.pallas_call(
        functools.partial(_cumsum_kernel, tile=tile),
        grid=(batch,),
        in_specs=[blk],
        out_specs=blk,
        out_shape=jax.ShapeDtypeStruct(x3.shape, F32),
        compiler_params=_cparams("parallel"),
        name="cumsum_seq",
    )(x3)


def _fox_prompt_kernel(q_ref, k_ref, v_ref, c_ref, o_ref, *, tq, scale):
    qi = pl.program_id(2)
    q = q_ref[...]
    sub = tq // LANES
    cq_rows = c_ref[0, pl.ds(qi * sub, sub), :]
    cq = jnp.concatenate([_col_bcast(cq_rows[i:i + 1, :]) for i in range(sub)], axis=0)
    cq = jnp.concatenate([cq] * sub, axis=1)

    def block(ki, carry, masked):
        m, l, acc = carry
        k0 = pl.multiple_of(ki * tq, tq)
        k = k_ref[pl.ds(k0, tq), :]
        v = v_ref[pl.ds(k0, tq), :]
        ck_rows = c_ref[0, pl.ds(ki * sub, sub), :]
        ck = jnp.concatenate([jnp.broadcast_to(ck_rows[j:j + 1, :], (tq, LANES)) for j in range(sub)], axis=1)
        s = lax.dot_general(q, k, NT_DIMS, preferred_element_type=F32) * scale
        s = s + cq - ck
        if masked:
            s = jnp.where(_iota((tq, tq), 1) <= _iota((tq, tq), 0), s, -jnp.inf)
        m_new = jnp.maximum(m, jnp.max(s, axis=-1, keepdims=True))
        alpha = jnp.exp(m - m_new)
        p = jnp.exp(s - m_new)
        l = alpha * l + jnp.sum(p, axis=-1, keepdims=True)
        acc = alpha * acc + _dot(p.astype(BF16), v)
        return m_new, l, acc

    init = (jnp.full((tq, 1), -jnp.inf, F32), jnp.zeros((tq, 1), F32), jnp.zeros((tq, q.shape[1]), F32))
    carry = lax.fori_loop(0, qi, lambda ki, cr: block(ki, cr, False), init)
    _, l, acc = block(qi, carry, True)
    o_ref[...] = (acc / l).astype(o_ref.dtype)


def fox_prompt(q, k, v, c_tab, *, batch, heads, tq):
    t, hd = q.shape
    dh = hd // heads
    seq = t // batch
    nq = seq // tq
    return pl.pallas_call(
        functools.partial(_fox_prompt_kernel, tq=tq, scale=1.0 / (dh ** 0.5)),
        grid=(batch, heads, nq),
        in_specs=[
            pl.BlockSpec((tq, dh), lambda b, h, i: (b * nq + i, h)),
            pl.BlockSpec((seq, dh), lambda b, h, i: (b, h)),
            pl.BlockSpec((seq, dh), lambda b, h, i: (b, h)),
            pl.BlockSpec((1, seq // LANES, LANES), lambda b, h, i: (b * heads + h, 0, 0)),
        ],
        out_specs=pl.BlockSpec((tq, dh), lambda b, h, i: (b * nq + i, h)),
        out_shape=jax.ShapeDtypeStruct((t, hd), BF16),
        compiler_params=_cparams("parallel", "parallel", "arbitrary"),
        name="fox_prompt",
    )(q, k, v, c_tab)


BIAS_PAGES_PER_STEP = 8
ATTN_PAGES_PER_STEP = 2


def _visit_page(pt, batch_row, visit, n_pages):
    return pt[batch_row * n_pages + (n_pages - 1 - visit)]


def _forget_bias_kernel(*refs, n_sub):
    lfn_ref = refs[1]
    lf_refs = refs[2:2 + n_sub]
    o_ref, run_ref = refs[2 + n_sub:]
    page = lf_refs[0].shape[1]

    @pl.when(pl.program_id(1) == 0)
    def _():
        run_ref[...] = lfn_ref[0]

    later = (_iota((page, page), 1) > _iota((page, page), 0)).astype(F32)
    run = run_ref[...]
    for j in range(n_sub):
        lf = lf_refs[j][0]
        o_ref[0, j] = _dot_exact(later, lf) + run
        run = run + jnp.sum(lf, axis=0, keepdims=True)
    run_ref[...] = run


def forget_bias(logf_new, cache_logf, page_table):
    batch, heads = logf_new.shape
    n_phys, page, _ = cache_logf.shape
    n_pages = page_table.shape[1]
    n_sub = BIAS_PAGES_PER_STEP

    def page_spec(j):
        return pl.BlockSpec((1, page, heads), lambda b, s, pt: (_visit_page(pt, b, s * n_sub + j, n_pages), 0, 0))

    grid_spec = pltpu.PrefetchScalarGridSpec(
        num_scalar_prefetch=1,
        grid=(batch, n_pages // n_sub),
        in_specs=[pl.BlockSpec((1, 1, heads), lambda b, s, pt: (b, 0, 0))] + [page_spec(j) for j in range(n_sub)],
        out_specs=pl.BlockSpec((1, n_sub, page, heads), lambda b, s, pt: (b, s, 0, 0)),
        scratch_shapes=[pltpu.VMEM((1, heads), F32)],
    )
    return pl.pallas_call(
        functools.partial(_forget_bias_kernel, n_sub=n_sub),
        grid_spec=grid_spec,
        out_shape=jax.ShapeDtypeStruct((batch, n_pages, page, heads), F32),
        compiler_params=_cparams("parallel", "arbitrary"),
        name="forget_bias",
    )(page_table.reshape(-1), logf_new.reshape(batch, 1, heads), *([cache_logf] * n_sub))


def _fox_sample_kernel(*refs, n_sub, scale):
    q_ref, kn_ref, vn_ref = refs[1:4]
    page_refs = refs[4:4 + 3 * n_sub]
    o_ref, m_ref, l_ref, acc_ref = refs[4 + 3 * n_sub:]
    step = pl.program_id(1)
    heads, dh = q_ref.shape[1:]
    rows = page_refs[0].shape[1] * heads
    own_head = _iota((heads, rows), 1) % heads == _iota((heads, rows), 0)

    @pl.when(step == 0)
    def _():
        m_ref[...] = jnp.full_like(m_ref, -jnp.inf)
        l_ref[...] = jnp.zeros_like(l_ref)
        acc_ref[...] = jnp.zeros_like(acc_ref)

    q = q_ref[0].astype(BF16)
    m, l, acc = m_ref[...], l_ref[...], acc_ref[...]
    for j in range(n_sub):
        ck_ref, cv_ref, bias_ref = page_refs[3 * j:3 * j + 3]
        k = ck_ref[0].reshape(rows, dh).astype(BF16)
        v = cv_ref[0].reshape(rows, dh).astype(BF16)
        s = lax.dot_general(q, k, NT_DIMS, preferred_element_type=F32) * scale + bias_ref[0, 0]
        s = jnp.where(own_head, s, -jnp.inf)
        m_new = jnp.maximum(m, jnp.max(s, axis=-1, keepdims=True))
        alpha = jnp.exp(m - m_new)
        pr = jnp.exp(s - m_new)
        l = alpha * l + jnp.sum(pr, axis=-1, keepdims=True)
        acc = alpha * acc + _dot(pr.astype(BF16), v)
        m = m_new
    m_ref[...], l_ref[...], acc_ref[...] = m, l, acc

    @pl.when(step == pl.num_programs(1) - 1)
    def _():
        qf = q.astype(F32)
        kn = kn_ref[0].astype(BF16).astype(F32)
        vn = vn_ref[0].astype(BF16).astype(F32)
        s_new = jnp.sum(qf * kn, axis=-1, keepdims=True) * scale
        m_fin = jnp.maximum(m, s_new)
        a_fin = jnp.exp(m - m_fin)
        p_new = jnp.exp(s_new - m_fin)
        l_fin = a_fin * l + p_new
        o_ref[0] = (a_fin * acc + p_new.astype(BF16).astype(F32) * vn) / l_fin


def fox_sample(q, k_new, v_new, logf_new, cache_k, cache_v, cache_logf, page_table):
    batch, heads, dh = q.shape
    page = cache_k.shape[1]
    n_pages = page_table.shape[1]
    n_sub = ATTN_PAGES_PER_STEP
    bias = forget_bias(logf_new, cache_logf, page_table)
    bias = bias.reshape(batch, n_pages, 1, page * heads)

    new_spec = pl.BlockSpec((1, heads, dh), lambda b, s, pt: (b, 0, 0))
    in_specs = [new_spec, new_spec, new_spec]
    operands = [q, k_new, v_new]
    for j in range(n_sub):
        phys = lambda b, s, pt, j=j: (_visit_page(pt, b, s * n_sub + j, n_pages), 0, 0, 0)
        in_specs += [
            pl.BlockSpec((1, page, heads, dh), phys),
            pl.BlockSpec((1, page, heads, dh), phys),
            pl.BlockSpec((1, 1, 1, page * heads), lambda b, s, pt, j=j: (b, s * n_sub + j, 0, 0)),
        ]
        operands += [cache_k, cache_v, bias]
    grid_spec = pltpu.PrefetchScalarGridSpec(
        num_scalar_prefetch=1,
        grid=(batch, n_pages // n_sub),
        in_specs=in_specs,
        out_specs=pl.BlockSpec((1, heads, dh), lambda b, s, pt: (b, 0, 0)),
        scratch_shapes=[pltpu.VMEM((heads, 1), F32), pltpu.VMEM((heads, 1), F32), pltpu.VMEM((heads, dh), F32)],
    )
    return pl.pallas_call(
        functools.partial(_fox_sample_kernel, n_sub=n_sub, scale=1.0 / (dh ** 0.5)),
        grid_spec=grid_spec,
        out_shape=jax.ShapeDtypeStruct((batch, heads, dh), F32),
        compiler_params=_cparams("parallel", "arbitrary"),
        name="fox_sample",
    )(page_table.reshape(-1), *operands)


def _tiles(m):
    return min(m, 512)


def _trunk(y, w, *, batch, mamba_mixer, fox_mixer):
    m, d = y.shape
    tm = _tiles(m)
    nw = w["norm_w"]
    extras = {}
    for layer in range(2):
        y = ffn_half(y, nw[layer, 0], nw[layer, 1], w["ffn_in"][layer][0], w["ffn_out"][layer][0], tm=tm, tf=256)
        if layer == 0:
            d_inner, conv_dim, nh = w["in_split"]
            z = norm_matmul(y, nw[0, 2], w["in_proj"], tm=tm, tn=PROJ_TN, cols=(0, d_inner))
            xbc = norm_matmul(y, nw[0, 2], w["in_proj"], tm=tm, tn=PROJ_TN, cols=(d_inner, conv_dim))
            dt_raw = norm_matmul(y, nw[0, 2], w["in_proj"], tm=tm, tn=PROJ_TN, cols=(d_inner + conv_dim, nh))
            mixed, extras["conv"], extras["ssm"] = mamba_mixer(z, xbc, dt_raw)
            y = matmul_norm_residual(mixed, w["out_proj"], nw[0, 3], y, tm=tm, tk=512)
        else:
            q = norm_matmul(y, nw[1, 2], w["w_q"], tm=tm, tn=PROJ_TN, out_dtypes=(BF16,))
            attn = fox_mixer(q, extras)
            y = matmul_norm_residual(attn, w["w_o"], nw[1, 3], y, tm=tm, tk=512)
        y = ffn_half(y, nw[layer, 4], nw[layer, 5], w["ffn_in"][layer][1], w["ffn_out"][layer][1], tm=tm, tf=256)
        if layer == 0:
            hd = w["w_q"].shape[1]
            extras["k"], extras["k_bf"] = norm_matmul(y, w["kv_norm_w"], w["w_kv"], tm=tm, tn=PROJ_TN, cols=(0, hd),
                                                      out_dtypes=(F32, BF16))
            extras["v"], extras["v_bf"] = norm_matmul(y, w["kv_norm_w"], w["w_kv"], tm=tm, tn=PROJ_TN, cols=(hd, hd),
                                                      out_dtypes=(F32, BF16))
            extras["logf"] = norm_matmul(y, w["kv_norm_w"], w["w_f"], tm=tm, tn=LANES, logsig_bias=w["b_f"])
    return y, extras


def kernel(x_prompt, x_sample, state_conv, state_ssm, cache_k, cache_v, cache_logf, page_table,
           norm_w, ffn_w_in, ffn_w_out, m_in_proj, m_conv_w, m_conv_b, m_dt_bias, m_A_log, m_D,
           m_gnorm_w, m_out_proj, kv_norm_w, w_kvf, b_fg, w_q, w_o):
    bp, seq, d = x_prompt.shape
    bs = x_sample.shape[0]
    heads = cache_k.shape[2]
    dh = cache_k.shape[3]
    hd = heads * dh
    nh = m_dt_bias.shape[1]
    d_inner = nh * SSM_HEAD_DIM
    conv_dim = m_conv_w.shape[2]

    pad_f = LANES - heads
    w = {
        "norm_w": norm_w,
        "ffn_in": [[ffn_w_in[l, i].astype(BF16) for i in range(2)] for l in range(2)],
        "ffn_out": [[ffn_w_out[l, i].astype(BF16) for i in range(2)] for l in range(2)],
        "in_proj": m_in_proj[0].astype(BF16),
        "in_split": (d_inner, conv_dim, nh),
        "out_proj": m_out_proj[0].astype(BF16),
        "kv_norm_w": kv_norm_w,
        "w_kv": w_kvf.astype(BF16),
        "w_f": jnp.pad(w_kvf[:, 2 * hd:], ((0, 0), (0, pad_f))).astype(BF16),
        "b_f": jnp.pad(b_fg, (0, pad_f)),
        "w_q": w_q[0].astype(BF16),
        "w_o": w_o[0].astype(BF16),
    }
    d_exp = jnp.repeat(m_D[0], SSM_HEAD_DIM)
    conv_w, conv_b = m_conv_w[0], m_conv_b[0]

    def mamba_prompt(z, xbc, dt_raw):
        act, tail = conv_prompt(xbc, conv_w, conv_b, batch=bp, tc=512)
        dt, acs = ssd_prep(dt_raw, m_dt_bias[0], m_A_log[0])
        yb, hfin = ssd_prompt(act, z, dt, acs, d_exp, m_gnorm_w[0], batch=bp)
        return yb, tail, hfin.reshape(bp, nh, SSM_HEAD_DIM, D_STATE)

    def fox_prompt_mixer(q, ex):
        logf3 = ex["logf"].reshape(bp, seq, LANES)
        c = cumsum_seq(logf3)[:, :, :heads]
        c_tab = c.transpose(0, 2, 1).reshape(bp * heads, seq // LANES, LANES)
        return fox_prompt(q, ex["k_bf"], ex["v_bf"], c_tab, batch=bp, heads=heads, tq=512)

    y_p, ex_p = _trunk(x_prompt.reshape(bp * seq, d), w, batch=bp,
                       mamba_mixer=mamba_prompt, fox_mixer=fox_prompt_mixer)

    def mamba_sample(z, xbc, dt_raw):
        st = state_conv[0].transpose(1, 0, 2)
        act, nst = conv_step(st, xbc, conv_w, conv_b, tc=2048)
        h_new, yg = ssd_step(state_ssm[0], act, z, dt_raw, m_dt_bias[0], m_A_log[0], d_exp, m_gnorm_w[0])
        yg = jnp.pad(yg, ((0, SAMPLE_ROWS - bs), (0, 0))).astype(BF16)
        return yg, nst.transpose(1, 0, 2), h_new

    def fox_sample_mixer(q, ex):
        per_head = lambda a: a[:bs].astype(F32).reshape(bs, heads, dh)
        attn = fox_sample(per_head(q), per_head(ex["k"]), per_head(ex["v"]), ex["logf"][:bs, :heads],
                          cache_k, cache_v, cache_logf, page_table)
        return jnp.pad(attn.reshape(bs, hd), ((0, SAMPLE_ROWS - bs), (0, 0))).astype(BF16)

    xs_rows = jnp.pad(x_sample.reshape(bs, d), ((0, SAMPLE_ROWS - bs), (0, 0)))
    y_s, ex_s = _trunk(xs_rows, w, batch=bs, mamba_mixer=mamba_sample, fox_mixer=fox_sample_mixer)

    return (
        y_p.reshape(bp, seq, d),
        y_s[:bs].reshape(bs, 1, d),
        ex_p["conv"][None],
        ex_p["ssm"][None],
        ex_p["k"].reshape(bp, seq, heads, dh),
        ex_p["v"].reshape(bp, seq, heads, dh),
        ex_p["logf"][:, :heads].reshape(bp, seq, heads),
        ex_s["conv"][None],
        ex_s["ssm"][None],
        ex_s["k"][:bs].reshape(bs, 1, heads, dh),
        ex_s["v"][:bs].reshape(bs, 1, heads, dh),
        ex_s["logf"][:bs, :heads].reshape(bs, 1, heads),
    )
```

```python
import functools

import jax
import jax.numpy as jnp
from jax import lax
from jax.experimental import pallas as pl
from jax.experimental.pallas import tpu as pltpu

F32 = jnp.float32
BF16 = jnp.bfloat16
HIGHEST = lax.Precision.HIGHEST
EPS = 1e-6

SSM_HEAD_DIM = 64
SSM_GROUPS = 8
D_STATE = 128
CONV_W = 4
CHUNK = 128
ATTN_HEAD_DIM = 128
PAGE_SIZE = 128

LANES = 128
SAMPLE_ROWS = 16
PROJ_TN = 1024
PROJ_TK = 1024
FFN_TF = 256
VMEM_LIMIT = 56 * 1024 * 1024

NT_DIMS = (((1,), (1,)), ((), ()))


def _cparams(*semantics):
    return pltpu.CompilerParams(dimension_semantics=semantics, vmem_limit_bytes=VMEM_LIMIT)


def _rms(x, w):
    return x * lax.rsqrt(jnp.mean(x * x, axis=-1, keepdims=True) + EPS) * w


def _silu(x):
    return x * jax.nn.sigmoid(x)


def _softplus(x):
    return jnp.maximum(x, 0.0) + jnp.log1p(jnp.exp(-jnp.abs(x)))


def _dot(a, b):
    return jnp.dot(a, b, preferred_element_type=F32)


def _dot_exact(a, b):
    return jnp.dot(a, b, precision=HIGHEST, preferred_element_type=F32)


def _iota(shape, axis):
    return lax.broadcasted_iota(jnp.int32, shape, axis)


def _col_bcast(row):
    return jnp.broadcast_to(row, (LANES, LANES)).T


def _ffn_kernel(x_ref, pre_ref, post_ref, wg_ref, wu_ref, wo_ref, o_ref, xn_ref):
    f = pl.program_id(1)

    @pl.when(f == 0)
    def _():
        xn_ref[...] = _rms(x_ref[...], pre_ref[...]).astype(BF16)
        o_ref[...] = jnp.zeros_like(o_ref)

    xn = xn_ref[...]
    g = _dot(xn, wg_ref[...])
    u = _dot(xn, wu_ref[...])
    h = (_silu(g) * u).astype(BF16)
    o_ref[...] += _dot(h, wo_ref[...])

    @pl.when(f == pl.num_programs(1) - 1)
    def _():
        o_ref[...] = x_ref[...] + 0.5 * _rms(o_ref[...], post_ref[...])


def ffn_half(x, pre_w, post_w, w_in, w_out, which, *, tm, tf):
    m, d = x.shape
    dff = w_out.shape[2]
    nf = dff // tf
    lay, half = which
    return pl.pallas_call(
        _ffn_kernel,
        grid=(m // tm, nf),
        in_specs=[
            pl.BlockSpec((tm, d), lambda i, f: (i, 0), pipeline_mode=pl.Buffered(1)),
            pl.BlockSpec((1, d), lambda i, f: (0, 0)),
            pl.BlockSpec((1, d), lambda i, f: (0, 0)),
            pl.BlockSpec((None, None, d, tf), lambda i, f: (lay, half, 0, f)),
            pl.BlockSpec((None, None, d, tf), lambda i, f: (lay, half, 0, f + nf)),
            pl.BlockSpec((None, None, tf, d), lambda i, f: (lay, half, f, 0)),
        ],
        out_specs=pl.BlockSpec((tm, d), lambda i, f: (i, 0)),
        out_shape=jax.ShapeDtypeStruct((m, d), F32),
        scratch_shapes=[pltpu.VMEM((tm, d), BF16)],
        compiler_params=_cparams("parallel", "arbitrary"),
        name="ffn_half",
    )(x, pre_w.reshape(1, d), post_w.reshape(1, d), w_in, w_in, w_out)


def _norm_matmul_kernel(*refs, has_bias, n_out):
    x_ref, nw_ref, w_ref = refs[:3]
    b_ref = refs[3] if has_bias else None
    o_refs = refs[3 + has_bias:3 + has_bias + n_out]
    xn_ref = refs[-1]

    @pl.when(pl.program_id(1) == 0)
    def _():
        xn_ref[...] = _rms(x_ref[...], nw_ref[...]).astype(BF16)

    acc = _dot(xn_ref[...], w_ref[...])
    if has_bias:
        acc = -_softplus(-(acc + b_ref[...]))
    for o_ref in o_refs:
        o_ref[...] = acc.astype(o_ref.dtype)


def norm_matmul(x, norm_w, w, *, tm, tn, cols=None, out_dtypes=(F32,), logsig_bias=None):
    m, d = x.shape
    col0, n = cols if cols is not None else (0, w.shape[1])
    tn = min(tn, n)
    first = col0 // tn
    assert first * tn == col0 and n % tn == 0 and m % tm == 0
    has_bias = logsig_bias is not None
    in_specs = [
        pl.BlockSpec((tm, d), lambda i, j: (i, 0), pipeline_mode=pl.Buffered(1)),
        pl.BlockSpec((1, d), lambda i, j: (0, 0)),
        pl.BlockSpec((d, tn), lambda i, j: (0, first + j)),
    ]
    args = [x, norm_w.reshape(1, d), w]
    if has_bias:
        in_specs.append(pl.BlockSpec((1, tn), lambda i, j: (0, j)))
        args.append(logsig_bias.reshape(1, n))
    outs = pl.pallas_call(
        functools.partial(_norm_matmul_kernel, has_bias=has_bias, n_out=len(out_dtypes)),
        grid=(m // tm, n // tn),
        in_specs=in_specs,
        out_specs=[pl.BlockSpec((tm, tn), lambda i, j: (i, j)) for _ in out_dtypes],
        out_shape=[jax.ShapeDtypeStruct((m, n), dt) for dt in out_dtypes],
        scratch_shapes=[pltpu.VMEM((tm, d), BF16)],
        compiler_params=_cparams("parallel", "arbitrary"),
        name="norm_matmul",
    )(*args)
    return outs[0] if len(out_dtypes) == 1 else outs


def _matmul_norm_res_kernel(a_ref, w_ref, post_ref, res_ref, o_ref):
    k = pl.program_id(1)

    @pl.when(k == 0)
    def _():
        o_ref[...] = jnp.zeros_like(o_ref)

    o_ref[...] += _dot(a_ref[...], w_ref[...])

    @pl.when(k == pl.num_programs(1) - 1)
    def _():
        o_ref[...] = res_ref[...] + _rms(o_ref[...], post_ref[...])


def matmul_norm_residual(a, w, post_w, res, *, tm, tk):
    m, kdim = a.shape
    n = w.shape[1]
    return pl.pallas_call(
        _matmul_norm_res_kernel,
        grid=(m // tm, kdim // tk),
        in_specs=[
            pl.BlockSpec((tm, tk), lambda i, k: (i, k)),
            pl.BlockSpec((tk, n), lambda i, k: (k, 0)),
            pl.BlockSpec((1, n), lambda i, k: (0, 0)),
            pl.BlockSpec((tm, n), lambda i, k: (i, 0), pipeline_mode=pl.Buffered(1)),
        ],
        out_specs=pl.BlockSpec((tm, n), lambda i, k: (i, 0)),
        out_shape=jax.ShapeDtypeStruct((m, n), F32),
        compiler_params=_cparams("parallel", "arbitrary"),
        name="matmul_norm_residual",
    )(a, w, post_w.reshape(1, n), res)


def _conv_prompt_kernel(x_ref, w_ref, b_ref, o_ref, tail_ref):
    x = x_ref[0]
    seq = x.shape[0]
    row = _iota(x.shape, 0)
    acc = b_ref[...]
    for tap in range(CONV_W):
        back = CONV_W - 1 - tap
        xs = x if back == 0 else jnp.where(row >= back, pltpu.roll(x, back, 0), 0.0)
        acc = acc + xs * w_ref[tap:tap + 1, :]
    o_ref[0] = _silu(acc)
    tail_ref[0] = x[seq - (CONV_W - 1):, :]


def conv_prompt(xbc, conv_w, conv_b, *, batch, tc):
    t, c = xbc.shape
    seq = t // batch
    x3 = xbc.reshape(batch, seq, c)
    act, tail = pl.pallas_call(
        _conv_prompt_kernel,
        grid=(batch, c // tc),
        in_specs=[
            pl.BlockSpec((1, seq, tc), lambda b, j: (b, 0, j)),
            pl.BlockSpec((CONV_W, tc), lambda b, j: (0, j)),
            pl.BlockSpec((1, tc), lambda b, j: (0, j)),
        ],
        out_specs=[
            pl.BlockSpec((1, seq, tc), lambda b, j: (b, 0, j)),
            pl.BlockSpec((1, CONV_W - 1, tc), lambda b, j: (b, 0, j)),
        ],
        out_shape=[
            jax.ShapeDtypeStruct((batch, seq, c), F32),
            jax.ShapeDtypeStruct((batch, CONV_W - 1, c), F32),
        ],
        compiler_params=_cparams("parallel", "parallel"),
        name="conv_prompt",
    )(x3, conv_w, conv_b.reshape(1, c))
    return act.reshape(t, c), tail


def _ssd_prep_kernel(dtr_ref, bias_ref, alog_ref, dt_ref, acs_ref):
    dt = _softplus(dtr_ref[...] + bias_ref[...])
    da = dt * (-jnp.exp(alog_ref[...]))
    lower = (_iota((CHUNK, CHUNK), 1) <= _iota((CHUNK, CHUNK), 0)).astype(F32)
    dt_ref[...] = dt
    acs_ref[...] = _dot_exact(lower, da)


def ssd_prep(dt_raw, dt_bias, a_log):
    t, nh = dt_raw.shape
    blk = pl.BlockSpec((CHUNK, nh), lambda c: (c, 0))
    vec = pl.BlockSpec((1, nh), lambda c: (0, 0))
    return pl.pallas_call(
        _ssd_prep_kernel,
        grid=(t // CHUNK,),
        in_specs=[blk, vec, vec],
        out_specs=[blk, blk],
        out_shape=[jax.ShapeDtypeStruct((t, nh), F32)] * 2,
        compiler_params=_cparams("parallel"),
        name="ssd_prep",
    )(dt_raw, dt_bias.reshape(1, nh), a_log.reshape(1, nh))


def _ssd_kernel(x_ref, b_ref, c_ref, z_ref, acsc_ref, acst_ref, dtt_ref, d_ref, gw_ref,
                y_ref, hfin_ref, ht_ref, yd_ref, *, heads_per_group):
    c = pl.program_id(2)
    p_n = SSM_HEAD_DIM
    pair = LANES // p_n
    assert CHUNK == D_STATE == LANES and heads_per_group % pair == 0

    @pl.when(c == 0)
    def _():
        ht_ref[...] = jnp.zeros_like(ht_ref)

    bm = b_ref[...]
    cm = c_ref[...]
    bm_t = bm.T
    cb = lax.dot_general(cm.astype(BF16), bm.astype(BF16), NT_DIMS, preferred_element_type=F32)
    causal = _iota((CHUNK, CHUNK), 1) <= _iota((CHUNK, CHUNK), 0)
    lane_head = _iota((CHUNK, LANES), 1) // p_n
    acsc = acsc_ref[0, 0]
    acst = acst_ref[0, 0]
    dtt = dtt_ref[0, 0]

    for slab in range(heads_per_group // pair):
        lanes = slice(slab * LANES, (slab + 1) * LANES)
        x_bf = x_ref[:, lanes].astype(BF16)
        ht = ht_ref[:, lanes]
        rhs = jnp.concatenate([x_bf, ht.astype(BF16)], axis=0)
        y_slab = s_slab = keep = None
        for k in range(pair):
            r = slab * pair + k
            a_col = jnp.broadcast_to(acsc[:, r:r + 1], (CHUNK, CHUNK))
            a_row = acst[r:r + 1, :]
            dt_row = dtt[r:r + 1, :]
            decay = jnp.exp(jnp.where(causal, a_col - a_row, -jnp.inf))
            within = cb * decay * dt_row
            carried = cm * jnp.exp(a_col)
            lhs = jnp.concatenate([within, carried], axis=1).astype(BF16)
            y_r = _dot(lhs, rhs)
            a_last = a_row[:, CHUNK - 1:CHUNK]
            to_end = jnp.exp(a_last - a_row) * dt_row
            s_r = _dot((bm_t * to_end).astype(BF16), x_bf)
            k_r = jnp.broadcast_to(jnp.exp(a_last), (D_STATE, LANES))
            if k == 0:
                y_slab, s_slab, keep = y_r, s_r, k_r
            else:
                mine = lane_head == k
                y_slab = jnp.where(mine, y_r, y_slab)
                s_slab = jnp.where(mine, s_r, s_slab)
                keep = jnp.where(mine, k_r, keep)
        yd_ref[:, lanes] = y_slab
        ht_ref[:, lanes] = ht * keep + s_slab

    y = yd_ref[...] + d_ref[...] * x_ref[...]
    y = y * _silu(z_ref[...])
    y_ref[...] = _rms(y, gw_ref[...]).astype(y_ref.dtype)

    @pl.when(c == pl.num_programs(2) - 1)
    def _():
        hfin_ref[0] = ht_ref[...].T


def ssd_prompt(xbc_act, z, dt, acs, d_exp, gnorm_w, *, batch):
    t, conv_dim = xbc_act.shape
    d_inner = z.shape[1]
    nh = dt.shape[1]
    groups = SSM_GROUPS
    r_n = nh // groups
    gp = d_inner // groups
    nc = t // batch // CHUNK
    n_chunks = t // CHUNK
    per_group = lambda a: a.reshape(n_chunks, CHUNK, groups, r_n).transpose(0, 2, 1, 3)
    acs_g = per_group(acs)
    acs_gt = acs_g.transpose(0, 1, 3, 2)
    dt_gt = per_group(dt).transpose(0, 1, 3, 2)
    xoff = d_inner // D_STATE
    row = lambda b, g, c: b * nc + c
    col_form = pl.BlockSpec((1, 1, CHUNK, r_n), lambda b, g, c: (row(b, g, c), g, 0, 0))
    row_form = pl.BlockSpec((1, 1, r_n, CHUNK), lambda b, g, c: (row(b, g, c), g, 0, 0))
    y, hfin = pl.pallas_call(
        functools.partial(_ssd_kernel, heads_per_group=r_n),
        grid=(batch, groups, nc),
        in_specs=[
            pl.BlockSpec((CHUNK, gp), lambda b, g, c: (row(b, g, c), g)),
            pl.BlockSpec((CHUNK, D_STATE), lambda b, g, c: (row(b, g, c), xoff + g)),
            pl.BlockSpec((CHUNK, D_STATE), lambda b, g, c: (row(b, g, c), xoff + groups + g)),
            pl.BlockSpec((CHUNK, gp), lambda b, g, c: (row(b, g, c), g)),
            col_form, row_form, row_form,
            pl.BlockSpec((1, gp), lambda b, g, c: (0, g)),
            pl.BlockSpec((1, gp), lambda b, g, c: (0, g)),
        ],
        out_specs=[
            pl.BlockSpec((CHUNK, gp), lambda b, g, c: (row(b, g, c), g)),
            pl.BlockSpec((1, gp, D_STATE), lambda b, g, c: (b, g, 0)),
        ],
        out_shape=[
            jax.ShapeDtypeStruct((t, d_inner), BF16),
            jax.ShapeDtypeStruct((batch, d_inner, D_STATE), F32),
        ],
        scratch_shapes=[pltpu.VMEM((D_STATE, gp), F32), pltpu.VMEM((CHUNK, gp), F32)],
        compiler_params=_cparams("parallel", "parallel", "arbitrary"),
        name="ssd_prompt",
    )(xbc_act, xbc_act, xbc_act, z, acs_g, acs_gt, dt_gt, d_exp.reshape(1, d_inner),
      gnorm_w.reshape(1, d_inner))
    return y, hfin


def _conv_step_kernel(st_ref, new_ref, w_ref, b_ref, act_ref, nst_ref):
    xn = new_ref[...]
    acc = b_ref[...]
    for tap in range(CONV_W - 1):
        acc = acc + st_ref[tap] * w_ref[tap:tap + 1, :]
        if tap > 0:
            nst_ref[tap - 1] = st_ref[tap]
    acc = acc + xn * w_ref[CONV_W - 1:CONV_W, :]
    nst_ref[CONV_W - 2] = xn
    act_ref[...] = _silu(acc)


def conv_step(state_t, xbc_new, conv_w, conv_b, *, tc):
    taps, batch, c = state_t.shape
    return pl.pallas_call(
        _conv_step_kernel,
        grid=(c // tc,),
        in_specs=[
            pl.BlockSpec((taps, batch, tc), lambda j: (0, 0, j)),
            pl.BlockSpec((batch, tc), lambda j: (0, j)),
            pl.BlockSpec((CONV_W, tc), lambda j: (0, j)),
            pl.BlockSpec((1, tc), lambda j: (0, j)),
        ],
        out_specs=[
            pl.BlockSpec((batch, tc), lambda j: (0, j)),
            pl.BlockSpec((taps, batch, tc), lambda j: (0, 0, j)),
        ],
        out_shape=[
            jax.ShapeDtypeStruct((batch, c), F32),
            jax.ShapeDtypeStruct((taps, batch, c), F32),
        ],
        compiler_params=_cparams("parallel"),
        name="conv_step",
    )(state_t, xbc_new, conv_w, conv_b.reshape(1, c))


def _ssd_step_kernel(h_ref, xs_ref, b_ref, c_ref, z_ref, dtr_ref, bias_ref, alog_ref, d_ref, gw_ref,
                     ho_ref, y_ref, dte_ref, dece_ref, *, heads_per_group):
    g = pl.program_id(0)
    r_n, p_n = heads_per_group, SSM_HEAD_DIM
    gp = r_n * p_n
    batch, nh = dtr_ref.shape
    rows_per_tile = LANES // p_n

    expand = (_iota((nh, gp), 1) // p_n + g * r_n == _iota((nh, gp), 0)).astype(F32)
    dt = _softplus(dtr_ref[...] + bias_ref[...])
    dte_ref[...] = _dot_exact(dt, expand)
    dece_ref[...] = _dot_exact(jnp.exp(dt * (-jnp.exp(alog_ref[...]))), expand)

    def body(b, carry):
        x = xs_ref[pl.ds(b, 1), :]
        bv = b_ref[pl.ds(b, 1), :]
        cv = c_ref[pl.ds(b, 1), :]
        xdt = x * dte_ref[pl.ds(b, 1), :]
        dec = dece_ref[pl.ds(b, 1), :]
        y_parts = []
        for j in range(gp // LANES):
            lanes = slice(j * LANES, (j + 1) * LANES)
            heads = pl.ds(j * rows_per_tile, rows_per_tile)
            h = h_ref[b, heads].reshape(LANES, D_STATE)
            h_new = h * _col_bcast(dec[:, lanes]) + _col_bcast(xdt[:, lanes]) * bv
            ho_ref[b, heads] = h_new.reshape(rows_per_tile, p_n, D_STATE)
            ycol = jnp.sum(h_new * cv, axis=-1, keepdims=True)
            y_parts.append(jnp.broadcast_to(ycol, (LANES, LANES)).T[0:1, :])
        y = jnp.concatenate(y_parts, axis=1) + d_ref[...] * x
        y = y * _silu(z_ref[pl.ds(b, 1), :])
        y_ref[pl.ds(b, 1), :] = _rms(y, gw_ref[...])
        return carry

    lax.fori_loop(0, batch, body, 0)


def ssd_step(h, xbc_act, z, dt_raw, dt_bias, a_log, d_exp, gnorm_w):
    batch, nh, p_n, n_n = h.shape
    d_inner = nh * p_n
    groups = SSM_GROUPS
    r_n = nh // groups
    gp = d_inner // groups
    xoff = d_inner // D_STATE
    return pl.pallas_call(
        functools.partial(_ssd_step_kernel, heads_per_group=r_n),
        grid=(groups,),
        in_specs=[
            pl.BlockSpec((batch, r_n, p_n, n_n), lambda g: (0, g, 0, 0)),
            pl.BlockSpec((batch, gp), lambda g: (0, g)),
            pl.BlockSpec((batch, D_STATE), lambda g: (0, xoff + g)),
            pl.BlockSpec((batch, D_STATE), lambda g: (0, xoff + groups + g)),
            pl.BlockSpec((batch, gp), lambda g: (0, g)),
            pl.BlockSpec((batch, nh), lambda g: (0, 0)),
            pl.BlockSpec((1, nh), lambda g: (0, 0)),
            pl.BlockSpec((1, nh), lambda g: (0, 0)),
            pl.BlockSpec((1, gp), lambda g: (0, g)),
            pl.BlockSpec((1, gp), lambda g: (0, g)),
        ],
        out_specs=[
            pl.BlockSpec((batch, r_n, p_n, n_n), lambda g: (0, g, 0, 0)),
            pl.BlockSpec((batch, gp), lambda g: (0, g)),
        ],
        out_shape=[
            jax.ShapeDtypeStruct(h.shape, F32),
            jax.ShapeDtypeStruct((batch, d_inner), F32),
        ],
        scratch_shapes=[pltpu.VMEM((batch, gp), F32), pltpu.VMEM((batch, gp), F32)],
        compiler_params=_cparams("parallel"),
        name="ssd_step",
    )(h, xbc_act, xbc_act, xbc_act, z, dt_raw, dt_bias.reshape(1, nh), a_log.reshape(1, nh),
      d_exp.reshape(1, d_inner), gnorm_w.reshape(1, d_inner))


def _cumsum_kernel(x_ref, o_ref, *, tile):
    seq = x_ref.shape[1]
    lower = (_iota((tile, tile), 1) <= _iota((tile, tile), 0)).astype(F32)
    carry = jnp.zeros((1, x_ref.shape[2]), F32)
    for i in range(seq // tile):
        cs = _dot_exact(lower, x_ref[0, i * tile:(i + 1) * tile, :]) + carry
        o_ref[0, i * tile:(i + 1) * tile, :] = cs
        carry = cs[tile - 1:tile, :]


def cumsum_seq(x3, *, tile=256):
    batch, seq, n = x3.shape
    blk = pl.BlockSpec((1, seq, n), lambda b: (b, 0, 0))
    return pl.pallas_call(
        functools.partial(_cumsum_kernel, tile=tile),
        grid=(batch,),
        in_specs=[blk],
        out_specs=blk,
        out_shape=jax.ShapeDtypeStruct(x3.shape, F32),
        compiler_params=_cparams("parallel"),
        name="cumsum_seq",
    )(x3)


def _fox_prompt_kernel(q_ref, k_ref, v_ref, c_ref, o_ref, *, tq, scale):
    q = q_ref[...]
    sub = tq // LANES
    n_q = k_ref.shape[0] // tq

    def block(ki, carry, cq, masked):
        m, l, acc = carry
        k = k_ref[ki * tq:(ki + 1) * tq, :]
        v = v_ref[ki * tq:(ki + 1) * tq, :]
        ck_rows = c_ref[0, ki * sub:(ki + 1) * sub, :]
        ck = jnp.concatenate([jnp.broadcast_to(ck_rows[j:j + 1, :], (tq, LANES)) for j in range(sub)], axis=1)
        s = lax.dot_general(q, k, NT_DIMS, preferred_element_type=F32) * scale
        s = s + cq - ck
        if masked:
            s = jnp.where(_iota((tq, tq), 1) <= _iota((tq, tq), 0), s, -jnp.inf)
        m_new = jnp.maximum(m, jnp.max(s, axis=-1, keepdims=True))
        alpha = jnp.exp(m - m_new)
        p = jnp.exp(s - m_new)
        l = alpha * l + jnp.sum(p, axis=-1, keepdims=True)
        acc = alpha * acc + _dot(p.astype(BF16), v)
        return m_new, l, acc

    for n_before in range(n_q):
        @pl.when(pl.program_id(2) == n_before)
        def _(n_before=n_before):
            cq_rows = c_ref[0, n_before * sub:(n_before + 1) * sub, :]
            cq = jnp.concatenate([_col_bcast(cq_rows[i:i + 1, :]) for i in range(sub)], axis=0)
            cq = jnp.concatenate([cq] * sub, axis=1)
            carry = (jnp.full((tq, 1), -jnp.inf, F32), jnp.zeros((tq, 1), F32), jnp.zeros((tq, q.shape[1]), F32))
            for ki in range(n_before):
                carry = block(ki, carry, cq, False)
            _, l, acc = block(n_before, carry, cq, True)
            o_ref[...] = (acc / l).astype(o_ref.dtype)


def fox_prompt(q, k, v, c_tab, *, batch, heads, tq):
    t, hd = q.shape
    dh = hd // heads
    seq = t // batch
    nq = seq // tq
    return pl.pallas_call(
        functools.partial(_fox_prompt_kernel, tq=tq, scale=1.0 / (dh ** 0.5)),
        grid=(batch, heads, nq),
        in_specs=[
            pl.BlockSpec((tq, dh), lambda b, h, i: (b * nq + i, h)),
            pl.BlockSpec((seq, dh), lambda b, h, i: (b, h)),
            pl.BlockSpec((seq, dh), lambda b, h, i: (b, h)),
            pl.BlockSpec((1, seq // LANES, LANES), lambda b, h, i: (b * heads + h, 0, 0)),
        ],
        out_specs=pl.BlockSpec((tq, dh), lambda b, h, i: (b * nq + i, h)),
        out_shape=jax.ShapeDtypeStruct((t, hd), BF16),
        compiler_params=_cparams("parallel", "parallel", "arbitrary"),
        name="fox_prompt",
    )(q, k, v, c_tab)


BIAS_PAGES_PER_STEP = 8
ATTN_PAGES_PER_STEP = 2


def _visit_page(pt, batch_row, visit, n_pages):
    return pt[batch_row * n_pages + (n_pages - 1 - visit)]


def _forget_bias_kernel(*refs, n_sub):
    lfn_ref = refs[1]
    lf_refs = refs[2:2 + n_sub]
    o_ref, run_ref = refs[2 + n_sub:]
    page = lf_refs[0].shape[1]

    @pl.when(pl.program_id(1) == 0)
    def _():
        run_ref[...] = lfn_ref[0]

    later = (_iota((page, page), 1) > _iota((page, page), 0)).astype(F32)
    run = run_ref[...]
    for j in range(n_sub):
        lf = lf_refs[j][0]
        o_ref[0, j] = _dot_exact(later, lf) + run
        run = run + jnp.sum(lf, axis=0, keepdims=True)
    run_ref[...] = run


def forget_bias(logf_new, cache_logf, page_table):
    batch, heads = logf_new.shape
    n_phys, page, _ = cache_logf.shape
    n_pages = page_table.shape[1]
    n_sub = BIAS_PAGES_PER_STEP

    def page_spec(j):
        return pl.BlockSpec((1, page, heads), lambda b, s, pt: (_visit_page(pt, b, s * n_sub + j, n_pages), 0, 0))

    grid_spec = pltpu.PrefetchScalarGridSpec(
        num_scalar_prefetch=1,
        grid=(batch, n_pages // n_sub),
        in_specs=[pl.BlockSpec((1, 1, heads), lambda b, s, pt: (b, 0, 0))] + [page_spec(j) for j in range(n_sub)],
        out_specs=pl.BlockSpec((1, n_sub, page, heads), lambda b, s, pt: (b, s, 0, 0)),
        scratch_shapes=[pltpu.VMEM((1, heads), F32)],
    )
    return pl.pallas_call(
        functools.partial(_forget_bias_kernel, n_sub=n_sub),
        grid_spec=grid_spec,
        out_shape=jax.ShapeDtypeStruct((batch, n_pages, page, heads), F32),
        compiler_params=_cparams("parallel", "arbitrary"),
        name="forget_bias",
    )(page_table.reshape(-1), logf_new.reshape(batch, 1, heads), *([cache_logf] * n_sub))


def _fox_sample_kernel(*refs, n_sub, scale):
    q_ref, kn_ref, vn_ref = refs[1:4]
    page_refs = refs[4:4 + 3 * n_sub]
    o_ref, m_ref, l_ref, acc_ref = refs[4 + 3 * n_sub:]
    step = pl.program_id(1)
    heads, dh = q_ref.shape[1:]
    rows = page_refs[0].shape[1] * heads
    own_head = _iota((heads, rows), 1) % heads == _iota((heads, rows), 0)

    @pl.when(step == 0)
    def _():
        m_ref[...] = jnp.full_like(m_ref, -jnp.inf)
        l_ref[...] = jnp.zeros_like(l_ref)
        acc_ref[...] = jnp.zeros_like(acc_ref)

    q = q_ref[0].astype(BF16)
    m, l, acc = m_ref[...], l_ref[...], acc_ref[...]
    for j in range(n_sub):
        ck_ref, cv_ref, bias_ref = page_refs[3 * j:3 * j + 3]
        k = ck_ref[0].reshape(rows, dh).astype(BF16)
        v = cv_ref[0].reshape(rows, dh).astype(BF16)
        s = lax.dot_general(q, k, NT_DIMS, preferred_element_type=F32) * scale + bias_ref[0, 0]
        s = jnp.where(own_head, s, -jnp.inf)
        m_new = jnp.maximum(m, jnp.max(s, axis=-1, keepdims=True))
        alpha = jnp.exp(m - m_new)
        pr = jnp.exp(s - m_new)
        l = alpha * l + jnp.sum(pr, axis=-1, keepdims=True)
        acc = alpha * acc + _dot(pr.astype(BF16), v)
        m = m_new
    m_ref[...], l_ref[...], acc_ref[...] = m, l, acc

    @pl.when(step == pl.num_programs(1) - 1)
    def _():
        qf = q.astype(F32)
        kn = kn_ref[0].astype(BF16).astype(F32)
        vn = vn_ref[0].astype(BF16).astype(F32)
        s_new = jnp.sum(qf * kn, axis=-1, keepdims=True) * scale
        m_fin = jnp.maximum(m, s_new)
        a_fin = jnp.exp(m - m_fin)
        p_new = jnp.exp(s_new - m_fin)
        l_fin = a_fin * l + p_new
        o_ref[0] = (a_fin * acc + p_new.astype(BF16).astype(F32) * vn) / l_fin


def fox_sample(q, k_new, v_new, logf_new, cache_k, cache_v, cache_logf, page_table):
    batch, heads, dh = q.shape
    page = cache_k.shape[1]
    n_pages = page_table.shape[1]
    n_sub = ATTN_PAGES_PER_STEP
    bias = forget_bias(logf_new, cache_logf, page_table)
    bias = bias.reshape(batch, n_pages, 1, page * heads)

    new_spec = pl.BlockSpec((1, heads, dh), lambda b, s, pt: (b, 0, 0))
    in_specs = [new_spec, new_spec, new_spec]
    operands = [q, k_new, v_new]
    for j in range(n_sub):
        phys = lambda b, s, pt, j=j: (_visit_page(pt, b, s * n_sub + j, n_pages), 0, 0, 0)
        in_specs += [
            pl.BlockSpec((1, page, heads, dh), phys),
            pl.BlockSpec((1, page, heads, dh), phys),
            pl.BlockSpec((1, 1, 1, page * heads), lambda b, s, pt, j=j: (b, s * n_sub + j, 0, 0)),
        ]
        operands += [cache_k, cache_v, bias]
    grid_spec = pltpu.PrefetchScalarGridSpec(
        num_scalar_prefetch=1,
        grid=(batch, n_pages // n_sub),
        in_specs=in_specs,
        out_specs=pl.BlockSpec((1, heads, dh), lambda b, s, pt: (b, 0, 0)),
        scratch_shapes=[pltpu.VMEM((heads, 1), F32), pltpu.VMEM((heads, 1), F32), pltpu.VMEM((heads, dh), F32)],
    )
    return pl.pallas_call(
        functools.partial(_fox_sample_kernel, n_sub=n_sub, scale=1.0 / (dh ** 0.5)),
        grid_spec=grid_spec,
        out_shape=jax.ShapeDtypeStruct((batch, heads, dh), F32),
        compiler_params=_cparams("parallel", "arbitrary"),
        name="fox_sample",
    )(page_table.reshape(-1), *operands)


def _tiles(m):
    return min(m, 512)


def _trunk(y, w, *, batch, mamba_mixer, fox_mixer):
    m, d = y.shape
    tm = _tiles(m)
    nw = w["norm_w"]
    extras = {}
    for layer in range(2):
        y = ffn_half(y, nw[layer, 0], nw[layer, 1], w["ffn_in"], w["ffn_out"], (layer, 0), tm=tm, tf=FFN_TF)
        if layer == 0:
            d_inner, conv_dim, nh = w["in_split"]
            z = norm_matmul(y, nw[0, 2], w["in_proj"], tm=tm, tn=PROJ_TN, cols=(0, d_inner))
            xbc = norm_matmul(y, nw[0, 2], w["in_proj"], tm=tm, tn=PROJ_TN, cols=(d_inner, conv_dim))
            dt_raw = norm_matmul(y, nw[0, 2], w["in_proj"], tm=tm, tn=PROJ_TN, cols=(d_inner + conv_dim, nh))
            mixed, extras["conv"], extras["ssm"] = mamba_mixer(z, xbc, dt_raw)
            y = matmul_norm_residual(mixed, w["out_proj"], nw[0, 3], y, tm=tm, tk=PROJ_TK)
        else:
            q = norm_matmul(y, nw[1, 2], w["w_q"], tm=tm, tn=PROJ_TN, out_dtypes=(BF16,))
            attn = fox_mixer(q, extras)
            y = matmul_norm_residual(attn, w["w_o"], nw[1, 3], y, tm=tm, tk=PROJ_TK)
        y = ffn_half(y, nw[layer, 4], nw[layer, 5], w["ffn_in"], w["ffn_out"], (layer, 1), tm=tm, tf=FFN_TF)
        if layer == 0:
            hd = w["w_q"].shape[1]
            extras["k"], extras["k_bf"] = norm_matmul(y, w["kv_norm_w"], w["w_kv"], tm=tm, tn=PROJ_TN, cols=(0, hd),
                                                      out_dtypes=(F32, BF16))
            extras["v"], extras["v_bf"] = norm_matmul(y, w["kv_norm_w"], w["w_kv"], tm=tm, tn=PROJ_TN, cols=(hd, hd),
                                                      out_dtypes=(F32, BF16))
            extras["logf"] = norm_matmul(y, w["kv_norm_w"], w["w_f"], tm=tm, tn=LANES, logsig_bias=w["b_f"])
    return y, extras


def kernel(x_prompt, x_sample, state_conv, state_ssm, cache_k, cache_v, cache_logf, page_table,
           norm_w, ffn_w_in, ffn_w_out, m_in_proj, m_conv_w, m_conv_b, m_dt_bias, m_A_log, m_D,
           m_gnorm_w, m_out_proj, kv_norm_w, w_kvf, b_fg, w_q, w_o):
    bp, seq, d = x_prompt.shape
    bs = x_sample.shape[0]
    heads = cache_k.shape[2]
    dh = cache_k.shape[3]
    hd = heads * dh
    nh = m_dt_bias.shape[1]
    d_inner = nh * SSM_HEAD_DIM
    conv_dim = m_conv_w.shape[2]

    pad_f = LANES - heads
    w = {
        "norm_w": norm_w,
        "ffn_in": ffn_w_in.astype(BF16),
        "ffn_out": ffn_w_out.astype(BF16),
        "in_proj": m_in_proj[0].astype(BF16),
        "in_split": (d_inner, conv_dim, nh),
        "out_proj": m_out_proj[0].astype(BF16),
        "kv_norm_w": kv_norm_w,
        "w_kv": w_kvf.astype(BF16),
        "w_f": jnp.pad(w_kvf[:, 2 * hd:], ((0, 0), (0, pad_f))).astype(BF16),
        "b_f": jnp.pad(b_fg, (0, pad_f)),
        "w_q": w_q[0].astype(BF16),
        "w_o": w_o[0].astype(BF16),
    }
    d_exp = jnp.repeat(m_D[0], SSM_HEAD_DIM)
    conv_w, conv_b = m_conv_w[0], m_conv_b[0]

    def mamba_prompt(z, xbc, dt_raw):
        act, tail = conv_prompt(xbc, conv_w, conv_b, batch=bp, tc=512)
        dt, acs = ssd_prep(dt_raw, m_dt_bias[0], m_A_log[0])
        yb, hfin = ssd_prompt(act, z, dt, acs, d_exp, m_gnorm_w[0], batch=bp)
        return yb, tail, hfin.reshape(bp, nh, SSM_HEAD_DIM, D_STATE)

    def fox_prompt_mixer(q, ex):
        logf3 = ex["logf"].reshape(bp, seq, LANES)
        c = cumsum_seq(logf3)[:, :, :heads]
        c_tab = c.transpose(0, 2, 1).reshape(bp * heads, seq // LANES, LANES)
        return fox_prompt(q, ex["k_bf"], ex["v_bf"], c_tab, batch=bp, heads=heads, tq=512)

    y_p, ex_p = _trunk(x_prompt.reshape(bp * seq, d), w, batch=bp,
                       mamba_mixer=mamba_prompt, fox_mixer=fox_prompt_mixer)

    def mamba_sample(z, xbc, dt_raw):
        st = state_conv[0].transpose(1, 0, 2)
        act, nst = conv_step(st, xbc, conv_w, conv_b, tc=2048)
        h_new, yg = ssd_step(state_ssm[0], act, z, dt_raw, m_dt_bias[0], m_A_log[0], d_exp, m_gnorm_w[0])
        yg = jnp.pad(yg, ((0, SAMPLE_ROWS - bs), (0, 0))).astype(BF16)
        return yg, nst.transpose(1, 0, 2), h_new

    def fox_sample_mixer(q, ex):
        per_head = lambda a: a[:bs].astype(F32).reshape(bs, heads, dh)
        attn = fox_sample(per_head(q), per_head(ex["k"]), per_head(ex["v"]), ex["logf"][:bs, :heads],
                          cache_k, cache_v, cache_logf, page_table)
        return jnp.pad(attn.reshape(bs, hd), ((0, SAMPLE_ROWS - bs), (0, 0))).astype(BF16)

    xs_rows = jnp.pad(x_sample.reshape(bs, d), ((0, SAMPLE_ROWS - bs), (0, 0)))
    y_s, ex_s = _trunk(xs_rows, w, batch=bs, mamba_mixer=mamba_sample, fox_mixer=fox_sample_mixer)

    return (
        y_p.reshape(bp, seq, d),
        y_s[:bs].reshape(bs, 1, d),
        ex_p["conv"][None],
        ex_p["ssm"][None],
        ex_p["k"].reshape(bp, seq, heads, dh),
        ex_p["v"].reshape(bp, seq, heads, dh),
        ex_p["logf"][:, :heads].reshape(bp, seq, heads),
        ex_s["conv"][None],
        ex_s["ssm"][None],
        ex_s["k"][:bs].reshape(bs, 1, heads, dh),
        ex_s["v"][:bs].reshape(bs, 1, heads, dh),
        ex_s["logf"][:bs, :heads].reshape(bs, 1, heads),
    )
```

```python
import functools

import jax
import jax.numpy as jnp
from jax import lax
from jax.experimental import pallas as pl
from jax.experimental.pallas import tpu as pltpu

F32 = jnp.float32
BF16 = jnp.bfloat16
HIGHEST = lax.Precision.HIGHEST
EPS = 1e-6

SSM_HEAD_DIM = 64
SSM_GROUPS = 8
D_STATE = 128
CONV_W = 4
CHUNK = 128
ATTN_HEAD_DIM = 128
PAGE_SIZE = 128

LANES = 128
SAMPLE_ROWS = 16
PROJ_TN = 1024
PROJ_TK = 512
FFN_TF = 256
VMEM_LIMIT = 56 * 1024 * 1024

NT_DIMS = (((1,), (1,)), ((), ()))


def _cparams(*semantics):
    return pltpu.CompilerParams(dimension_semantics=semantics, vmem_limit_bytes=VMEM_LIMIT)


def _rms(x, w):
    return x * lax.rsqrt(jnp.mean(x * x, axis=-1, keepdims=True) + EPS) * w


def _silu(x):
    return x * jax.nn.sigmoid(x)


def _softplus(x):
    return jnp.maximum(x, 0.0) + jnp.log1p(jnp.exp(-jnp.abs(x)))


def _dot(a, b):
    return jnp.dot(a, b, preferred_element_type=F32)


def _dot_exact(a, b):
    return jnp.dot(a, b, precision=HIGHEST, preferred_element_type=F32)


def _iota(shape, axis):
    return lax.broadcasted_iota(jnp.int32, shape, axis)


def _col_bcast(row):
    return jnp.broadcast_to(row, (LANES, LANES)).T


def _ffn_kernel(*refs, emit_bf16):
    x_ref, pre_ref, post_ref, wg_ref, wu_ref, wo_ref, o_ref = refs[:7]
    xn_ref = refs[-1]
    f = pl.program_id(1)

    @pl.when(f == 0)
    def _():
        xn_ref[...] = _rms(x_ref[...], pre_ref[...]).astype(BF16)
        o_ref[...] = jnp.zeros_like(o_ref)

    wg, wu, wo = (r[...].astype(BF16) for r in (wg_ref, wu_ref, wo_ref))
    if emit_bf16:
        for w_tile, out_ref in zip((wg, wu, wo), refs[7:10]):
            out_ref[...] = w_tile
    xn = xn_ref[...]
    h = (_silu(_dot(xn, wg)) * _dot(xn, wu)).astype(BF16)
    o_ref[...] += _dot(h, wo)

    @pl.when(f == pl.num_programs(1) - 1)
    def _():
        o_ref[...] = x_ref[...] + 0.5 * _rms(o_ref[...], post_ref[...])


def ffn_half(x, pre_w, post_w, w_in, w_out, which, *, tm, tf):
    m, d = x.shape
    emit = which is not None
    if emit:
        assert m == tm
        lay, half = which
        dff = w_out.shape[2]
        nf = dff // tf
        w_specs = [
            pl.BlockSpec((None, None, d, tf), lambda i, f: (lay, half, 0, f)),
            pl.BlockSpec((None, None, d, tf), lambda i, f: (lay, half, 0, f + nf)),
            pl.BlockSpec((None, None, tf, d), lambda i, f: (lay, half, f, 0)),
        ]
        w_args = (w_in, w_in, w_out)
    else:
        dff = w_out.shape[0]
        w_specs = [
            pl.BlockSpec((d, tf), lambda i, f: (0, f)),
            pl.BlockSpec((d, tf), lambda i, f: (0, f)),
            pl.BlockSpec((tf, d), lambda i, f: (f, 0)),
        ]
        w_args = (*w_in, w_out)
    out_specs = [pl.BlockSpec((tm, d), lambda i, f: (i, 0))]
    out_shape = [jax.ShapeDtypeStruct((m, d), F32)]
    if emit:
        out_specs += [
            pl.BlockSpec((d, tf), lambda i, f: (0, f)),
            pl.BlockSpec((d, tf), lambda i, f: (0, f)),
            pl.BlockSpec((tf, d), lambda i, f: (f, 0)),
        ]
        out_shape += [jax.ShapeDtypeStruct((d, dff), BF16)] * 2 + [jax.ShapeDtypeStruct((dff, d), BF16)]
    outs = pl.pallas_call(
        functools.partial(_ffn_kernel, emit_bf16=emit),
        grid=(m // tm, dff // tf),
        in_specs=[
            pl.BlockSpec((tm, d), lambda i, f: (i, 0), pipeline_mode=pl.Buffered(1)),
            pl.BlockSpec((1, d), lambda i, f: (0, 0)),
            pl.BlockSpec((1, d), lambda i, f: (0, 0)),
            *w_specs,
        ],
        out_specs=out_specs,
        out_shape=out_shape,
        scratch_shapes=[pltpu.VMEM((tm, d), BF16)],
        compiler_params=_cparams("parallel", "arbitrary"),
        name="ffn_half",
    )(x, pre_w.reshape(1, d), post_w.reshape(1, d), *w_args)
    return (outs[0], ((outs[1], outs[2]), outs[3])) if emit else outs[0]


def _norm_matmul_kernel(*refs, has_bias, n_out, emit_bf16):
    x_ref, nw_ref, w_ref = refs[:3]
    b_ref = refs[3] if has_bias else None
    o_refs = refs[3 + has_bias:3 + has_bias + n_out]
    xn_ref = refs[-1]

    @pl.when(pl.program_id(1) == 0)
    def _():
        xn_ref[...] = _rms(x_ref[...], nw_ref[...]).astype(BF16)

    w_tile = w_ref[...].astype(BF16)
    if emit_bf16:
        refs[3 + has_bias + n_out][...] = w_tile
    acc = _dot(xn_ref[...], w_tile)
    if has_bias:
        acc = -_softplus(-(acc + b_ref[...]))
    for o_ref in o_refs:
        o_ref[...] = acc.astype(o_ref.dtype)


def norm_matmul(x, norm_w, w, *, tm, tn, cols=None, out_dtypes=(F32,), logsig_bias=None, emit_bf16=False):
    m, d = x.shape
    col0, n = cols if cols is not None else (0, w.shape[1])
    tn = min(tn, n)
    first = col0 // tn
    assert first * tn == col0 and n % tn == 0 and m % tm == 0 and (m == tm or not emit_bf16)
    has_bias = logsig_bias is not None
    in_specs = [
        pl.BlockSpec((tm, d), lambda i, j: (i, 0)),
        pl.BlockSpec((1, d), lambda i, j: (0, 0)),
        pl.BlockSpec((d, tn), lambda i, j: (0, first + j)),
    ]
    args = [x, norm_w.reshape(1, d), w]
    if has_bias:
        in_specs.append(pl.BlockSpec((1, tn), lambda i, j: (0, j)))
        args.append(logsig_bias.reshape(1, n))
    out_specs = [pl.BlockSpec((tm, tn), lambda i, j: (i, j)) for _ in out_dtypes]
    out_shape = [jax.ShapeDtypeStruct((m, n), dt) for dt in out_dtypes]
    if emit_bf16:
        out_specs.append(pl.BlockSpec((d, tn), lambda i, j: (0, j)))
        out_shape.append(jax.ShapeDtypeStruct((d, n), BF16))
    outs = pl.pallas_call(
        functools.partial(_norm_matmul_kernel, has_bias=has_bias, n_out=len(out_dtypes), emit_bf16=emit_bf16),
        grid=(m // tm, n // tn),
        in_specs=in_specs,
        out_specs=out_specs,
        out_shape=out_shape,
        scratch_shapes=[pltpu.VMEM((tm, d), BF16)],
        compiler_params=_cparams("parallel", "arbitrary"),
        name="norm_matmul",
    )(*args)
    return outs[0] if len(outs) == 1 else outs


def _matmul_norm_res_kernel(a_ref, w_ref, post_ref, res_ref, o_ref):
    k = pl.program_id(1)

    @pl.when(k == 0)
    def _():
        o_ref[...] = jnp.zeros_like(o_ref)

    o_ref[...] += _dot(a_ref[...], w_ref[...])

    @pl.when(k == pl.num_programs(1) - 1)
    def _():
        o_ref[...] = res_ref[...] + _rms(o_ref[...], post_ref[...])


def matmul_norm_residual(a, w, post_w, res, *, tm, tk):
    m, kdim = a.shape
    n = w.shape[1]
    return pl.pallas_call(
        _matmul_norm_res_kernel,
        grid=(m // tm, kdim // tk),
        in_specs=[
            pl.BlockSpec((tm, tk), lambda i, k: (i, k)),
            pl.BlockSpec((tk, n), lambda i, k: (k, 0)),
            pl.BlockSpec((1, n), lambda i, k: (0, 0)),
            pl.BlockSpec((tm, n), lambda i, k: (i, 0)),
        ],
        out_specs=pl.BlockSpec((tm, n), lambda i, k: (i, 0)),
        out_shape=jax.ShapeDtypeStruct((m, n), F32),
        compiler_params=_cparams("parallel", "arbitrary"),
        name="matmul_norm_residual",
    )(a, w, post_w.reshape(1, n), res)


def _conv_prompt_kernel(x_ref, w_ref, b_ref, o_ref, tail_ref):
    x = x_ref[0]
    seq = x.shape[0]
    row = _iota(x.shape, 0)
    acc = b_ref[...]
    for tap in range(CONV_W):
        back = CONV_W - 1 - tap
        xs = x if back == 0 else jnp.where(row >= back, pltpu.roll(x, back, 0), 0.0)
        acc = acc + xs * w_ref[tap:tap + 1, :]
    o_ref[0] = _silu(acc)
    tail_ref[0] = x[seq - (CONV_W - 1):, :]


def conv_prompt(xbc, conv_w, conv_b, *, batch, tc):
    t, c = xbc.shape
    seq = t // batch
    x3 = xbc.reshape(batch, seq, c)
    act, tail = pl.pallas_call(
        _conv_prompt_kernel,
        grid=(batch, c // tc),
        in_specs=[
            pl.BlockSpec((1, seq, tc), lambda b, j: (b, 0, j)),
            pl.BlockSpec((CONV_W, tc), lambda b, j: (0, j)),
            pl.BlockSpec((1, tc), lambda b, j: (0, j)),
        ],
        out_specs=[
            pl.BlockSpec((1, seq, tc), lambda b, j: (b, 0, j)),
            pl.BlockSpec((1, CONV_W - 1, tc), lambda b, j: (b, 0, j)),
        ],
        out_shape=[
            jax.ShapeDtypeStruct((batch, seq, c), F32),
            jax.ShapeDtypeStruct((batch, CONV_W - 1, c), F32),
        ],
        compiler_params=_cparams("parallel", "parallel"),
        name="conv_prompt",
    )(x3, conv_w, conv_b.reshape(1, c))
    return act.reshape(t, c), tail


def _ssd_prep_kernel(dtr_ref, bias_ref, alog_ref, dt_ref, acs_ref):
    dt = _softplus(dtr_ref[...] + bias_ref[...])
    da = dt * (-jnp.exp(alog_ref[...]))
    lower = (_iota((CHUNK, CHUNK), 1) <= _iota((CHUNK, CHUNK), 0)).astype(F32)
    dt_ref[...] = dt
    acs_ref[...] = _dot_exact(lower, da)


def ssd_prep(dt_raw, dt_bias, a_log):
    t, nh = dt_raw.shape
    blk = pl.BlockSpec((CHUNK, nh), lambda c: (c, 0))
    vec = pl.BlockSpec((1, nh), lambda c: (0, 0))
    return pl.pallas_call(
        _ssd_prep_kernel,
        grid=(t // CHUNK,),
        in_specs=[blk, vec, vec],
        out_specs=[blk, blk],
        out_shape=[jax.ShapeDtypeStruct((t, nh), F32)] * 2,
        compiler_params=_cparams("parallel"),
        name="ssd_prep",
    )(dt_raw, dt_bias.reshape(1, nh), a_log.reshape(1, nh))


def _ssd_kernel(x_ref, b_ref, c_ref, z_ref, acsc_ref, acst_ref, dtt_ref, d_ref, gw_ref,
                y_ref, hfin_ref, ht_ref, yd_ref, *, heads_per_group):
    c = pl.program_id(2)
    p_n = SSM_HEAD_DIM
    pair = LANES // p_n
    assert CHUNK == D_STATE == LANES and heads_per_group % pair == 0

    @pl.when(c == 0)
    def _():
        ht_ref[...] = jnp.zeros_like(ht_ref)

    bm = b_ref[...]
    cm = c_ref[...]
    bm_t = bm.T
    cb = lax.dot_general(cm.astype(BF16), bm.astype(BF16), NT_DIMS, preferred_element_type=F32)
    causal = _iota((CHUNK, CHUNK), 1) <= _iota((CHUNK, CHUNK), 0)
    lane_head = _iota((CHUNK, LANES), 1) // p_n
    acsc = acsc_ref[0, 0]
    acst = acst_ref[0, 0]
    dtt = dtt_ref[0, 0]

    for slab in range(heads_per_group // pair):
        lanes = slice(slab * LANES, (slab + 1) * LANES)
        x_bf = x_ref[:, lanes].astype(BF16)
        ht = ht_ref[:, lanes]
        rhs = jnp.concatenate([x_bf, ht.astype(BF16)], axis=0)
        y_slab = s_slab = keep = None
        for k in range(pair):
            r = slab * pair + k
            a_col = jnp.broadcast_to(acsc[:, r:r + 1], (CHUNK, CHUNK))
            a_row = acst[r:r + 1, :]
            dt_row = dtt[r:r + 1, :]
            decay = jnp.exp(jnp.where(causal, a_col - a_row, -jnp.inf))
            within = cb * decay * dt_row
            carried = cm * jnp.exp(a_col)
            lhs = jnp.concatenate([within, carried], axis=1).astype(BF16)
            y_r = _dot(lhs, rhs)
            a_last = a_row[:, CHUNK - 1:CHUNK]
            to_end = jnp.exp(a_last - a_row) * dt_row
            s_r = _dot((bm_t * to_end).astype(BF16), x_bf)
            k_r = jnp.broadcast_to(jnp.exp(a_last), (D_STATE, LANES))
            if k == 0:
                y_slab, s_slab, keep = y_r, s_r, k_r
            else:
                mine = lane_head == k
                y_slab = jnp.where(mine, y_r, y_slab)
                s_slab = jnp.where(mine, s_r, s_slab)
                keep = jnp.where(mine, k_r, keep)
        yd_ref[:, lanes] = y_slab
        ht_ref[:, lanes] = ht * keep + s_slab

    y = yd_ref[...] + d_ref[...] * x_ref[...]
    y = y * _silu(z_ref[...])
    y_ref[...] = _rms(y, gw_ref[...]).astype(y_ref.dtype)

    @pl.when(c == pl.num_programs(2) - 1)
    def _():
        hfin_ref[0] = ht_ref[...].T


def ssd_prompt(xbc_act, z, dt, acs, d_exp, gnorm_w, *, batch):
    t, conv_dim = xbc_act.shape
    d_inner = z.shape[1]
    nh = dt.shape[1]
    groups = SSM_GROUPS
    r_n = nh // groups
    gp = d_inner // groups
    nc = t // batch // CHUNK
    n_chunks = t // CHUNK
    per_group = lambda a: a.reshape(n_chunks, CHUNK, groups, r_n).transpose(0, 2, 1, 3)
    acs_g = per_group(acs)
    acs_gt = acs_g.transpose(0, 1, 3, 2)
    dt_gt = per_group(dt).transpose(0, 1, 3, 2)
    xoff = d_inner // D_STATE
    row = lambda b, g, c: b * nc + c
    col_form = pl.BlockSpec((1, 1, CHUNK, r_n), lambda b, g, c: (row(b, g, c), g, 0, 0))
    row_form = pl.BlockSpec((1, 1, r_n, CHUNK), lambda b, g, c: (row(b, g, c), g, 0, 0))
    y, hfin = pl.pallas_call(
        functools.partial(_ssd_kernel, heads_per_group=r_n),
        grid=(batch, groups, nc),
        in_specs=[
            pl.BlockSpec((CHUNK, gp), lambda b, g, c: (row(b, g, c), g)),
            pl.BlockSpec((CHUNK, D_STATE), lambda b, g, c: (row(b, g, c), xoff + g)),
            pl.BlockSpec((CHUNK, D_STATE), lambda b, g, c: (row(b, g, c), xoff + groups + g)),
            pl.BlockSpec((CHUNK, gp), lambda b, g, c: (row(b, g, c), g)),
            col_form, row_form, row_form,
            pl.BlockSpec((1, gp), lambda b, g, c: (0, g)),
            pl.BlockSpec((1, gp), lambda b, g, c: (0, g)),
        ],
        out_specs=[
            pl.BlockSpec((CHUNK, gp), lambda b, g, c: (row(b, g, c), g)),
            pl.BlockSpec((1, gp, D_STATE), lambda b, g, c: (b, g, 0)),
        ],
        out_shape=[
            jax.ShapeDtypeStruct((t, d_inner), BF16),
            jax.ShapeDtypeStruct((batch, d_inner, D_STATE), F32),
        ],
        scratch_shapes=[pltpu.VMEM((D_STATE, gp), F32), pltpu.VMEM((CHUNK, gp), F32)],
        compiler_params=_cparams("parallel", "parallel", "arbitrary"),
        name="ssd_prompt",
    )(xbc_act, xbc_act, xbc_act, z, acs_g, acs_gt, dt_gt, d_exp.reshape(1, d_inner),
      gnorm_w.reshape(1, d_inner))
    return y, hfin


def _conv_step_kernel(st_ref, new_ref, w_ref, b_ref, act_ref, nst_ref):
    xn = new_ref[...]
    acc = b_ref[...]
    for tap in range(CONV_W - 1):
        acc = acc + st_ref[tap] * w_ref[tap:tap + 1, :]
        if tap > 0:
            nst_ref[tap - 1] = st_ref[tap]
    acc = acc + xn * w_ref[CONV_W - 1:CONV_W, :]
    nst_ref[CONV_W - 2] = xn
    act_ref[...] = _silu(acc)


def conv_step(state_t, xbc_new, conv_w, conv_b, *, tc):
    taps, batch, c = state_t.shape
    return pl.pallas_call(
        _conv_step_kernel,
        grid=(c // tc,),
        in_specs=[
            pl.BlockSpec((taps, batch, tc), lambda j: (0, 0, j)),
            pl.BlockSpec((batch, tc), lambda j: (0, j)),
            pl.BlockSpec((CONV_W, tc), lambda j: (0, j)),
            pl.BlockSpec((1, tc), lambda j: (0, j)),
        ],
        out_specs=[
            pl.BlockSpec((batch, tc), lambda j: (0, j)),
            pl.BlockSpec((taps, batch, tc), lambda j: (0, 0, j)),
        ],
        out_shape=[
            jax.ShapeDtypeStruct((batch, c), F32),
            jax.ShapeDtypeStruct((taps, batch, c), F32),
        ],
        compiler_params=_cparams("parallel"),
        name="conv_step",
    )(state_t, xbc_new, conv_w, conv_b.reshape(1, c))


def _ssd_step_kernel(h_ref, xs_ref, b_ref, c_ref, z_ref, dtr_ref, bias_ref, alog_ref, d_ref, gw_ref,
                     ho_ref, y_ref, dte_ref, dece_ref, *, heads_per_group):
    g = pl.program_id(0)
    r_n, p_n = heads_per_group, SSM_HEAD_DIM
    gp = r_n * p_n
    batch, nh = dtr_ref.shape
    rows_per_tile = LANES // p_n

    expand = (_iota((nh, gp), 1) // p_n + g * r_n == _iota((nh, gp), 0)).astype(F32)
    dt = _softplus(dtr_ref[...] + bias_ref[...])
    dte_ref[...] = _dot_exact(dt, expand)
    dece_ref[...] = _dot_exact(jnp.exp(dt * (-jnp.exp(alog_ref[...]))), expand)

    def body(b, carry):
        x = xs_ref[pl.ds(b, 1), :]
        bv = b_ref[pl.ds(b, 1), :]
        cv = c_ref[pl.ds(b, 1), :]
        xdt = x * dte_ref[pl.ds(b, 1), :]
        dec = dece_ref[pl.ds(b, 1), :]
        y_parts = []
        for j in range(gp // LANES):
            lanes = slice(j * LANES, (j + 1) * LANES)
            heads = pl.ds(j * rows_per_tile, rows_per_tile)
            h = h_ref[b, heads].reshape(LANES, D_STATE)
            h_new = h * _col_bcast(dec[:, lanes]) + _col_bcast(xdt[:, lanes]) * bv
            ho_ref[b, heads] = h_new.reshape(rows_per_tile, p_n, D_STATE)
            ycol = jnp.sum(h_new * cv, axis=-1, keepdims=True)
            y_parts.append(jnp.broadcast_to(ycol, (LANES, LANES)).T[0:1, :])
        y = jnp.concatenate(y_parts, axis=1) + d_ref[...] * x
        y = y * _silu(z_ref[pl.ds(b, 1), :])
        y_ref[pl.ds(b, 1), :] = _rms(y, gw_ref[...])
        return carry

    lax.fori_loop(0, batch, body, 0)


def ssd_step(h, xbc_act, z, dt_raw, dt_bias, a_log, d_exp, gnorm_w):
    batch, nh, p_n, n_n = h.shape
    d_inner = nh * p_n
    groups = SSM_GROUPS
    r_n = nh // groups
    gp = d_inner // groups
    xoff = d_inner // D_STATE
    return pl.pallas_call(
        functools.partial(_ssd_step_kernel, heads_per_group=r_n),
        grid=(groups,),
        in_specs=[
            pl.BlockSpec((batch, r_n, p_n, n_n), lambda g: (0, g, 0, 0)),
            pl.BlockSpec((batch, gp), lambda g: (0, g)),
            pl.BlockSpec((batch, D_STATE), lambda g: (0, xoff + g)),
            pl.BlockSpec((batch, D_STATE), lambda g: (0, xoff + groups + g)),
            pl.BlockSpec((batch, gp), lambda g: (0, g)),
            pl.BlockSpec((batch, nh), lambda g: (0, 0)),
            pl.BlockSpec((1, nh), lambda g: (0, 0)),
            pl.BlockSpec((1, nh), lambda g: (0, 0)),
            pl.BlockSpec((1, gp), lambda g: (0, g)),
            pl.BlockSpec((1, gp), lambda g: (0, g)),
        ],
        out_specs=[
            pl.BlockSpec((batch, r_n, p_n, n_n), lambda g: (0, g, 0, 0)),
            pl.BlockSpec((batch, gp), lambda g: (0, g)),
        ],
        out_shape=[
            jax.ShapeDtypeStruct(h.shape, F32),
            jax.ShapeDtypeStruct((batch, d_inner), F32),
        ],
        scratch_shapes=[pltpu.VMEM((batch, gp), F32), pltpu.VMEM((batch, gp), F32)],
        compiler_params=_cparams("parallel"),
        name="ssd_step",
    )(h, xbc_act, xbc_act, xbc_act, z, dt_raw, dt_bias.reshape(1, nh), a_log.reshape(1, nh),
      d_exp.reshape(1, d_inner), gnorm_w.reshape(1, d_inner))


def _cumsum_kernel(x_ref, o_ref, *, tile):
    seq = x_ref.shape[1]
    lower = (_iota((tile, tile), 1) <= _iota((tile, tile), 0)).astype(F32)
    carry = jnp.zeros((1, x_ref.shape[2]), F32)
    for i in range(seq // tile):
        cs = _dot_exact(lower, x_ref[0, i * tile:(i + 1) * tile, :]) + carry
        o_ref[0, i * tile:(i + 1) * tile, :] = cs
        carry = cs[tile - 1:tile, :]


def cumsum_seq(x3, *, tile=256):
    batch, seq, n = x3.shape
    blk = pl.BlockSpec((1, seq, n), lambda b: (b, 0, 0))
    return pl.pallas_call(
        functools.partial(_cumsum_kernel, tile=tile),
        grid=(batch,),
        in_specs=[blk],
        out_specs=blk,
        out_shape=jax.ShapeDtypeStruct(x3.shape, F32),
        compiler_params=_cparams("parallel"),
        name="cumsum_seq",
    )(x3)


def _fox_prompt_kernel(q_ref, k_ref, v_ref, c_ref, o_ref, *, tq, scale):
    q = q_ref[...]
    sub = tq // LANES
    n_q = k_ref.shape[0] // tq

    def block(ki, carry, cq, masked):
        m, l, acc = carry
        k = k_ref[ki * tq:(ki + 1) * tq, :]
        v = v_ref[ki * tq:(ki + 1) * tq, :]
        ck_rows = c_ref[0, ki * sub:(ki + 1) * sub, :]
        ck = jnp.concatenate([jnp.broadcast_to(ck_rows[j:j + 1, :], (tq, LANES)) for j in range(sub)], axis=1)
        s = lax.dot_general(q, k, NT_DIMS, preferred_element_type=F32) * scale
        s = s + cq - ck
        if masked:
            s = jnp.where(_iota((tq, tq), 1) <= _iota((tq, tq), 0), s, -jnp.inf)
        m_new = jnp.maximum(m, jnp.max(s, axis=-1, keepdims=True))
        alpha = jnp.exp(m - m_new)
        p = jnp.exp(s - m_new)
        l = alpha * l + jnp.sum(p, axis=-1, keepdims=True)
        acc = alpha * acc + _dot(p.astype(BF16), v)
        return m_new, l, acc

    for n_before in range(n_q):
        @pl.when(pl.program_id(2) == n_before)
        def _(n_before=n_before):
            cq_rows = c_ref[0, n_before * sub:(n_before + 1) * sub, :]
            cq = jnp.concatenate([_col_bcast(cq_rows[i:i + 1, :]) for i in range(sub)], axis=0)
            cq = jnp.concatenate([cq] * sub, axis=1)
            carry = (jnp.full((tq, 1), -jnp.inf, F32), jnp.zeros((tq, 1), F32), jnp.zeros((tq, q.shape[1]), F32))
            for ki in range(n_before):
                carry = block(ki, carry, cq, False)
            _, l, acc = block(n_before, carry, cq, True)
            o_ref[...] = (acc / l).astype(o_ref.dtype)


def fox_prompt(q, k, v, c_tab, *, batch, heads, tq):
    t, hd = q.shape
    dh = hd // heads
    seq = t // batch
    nq = seq // tq
    return pl.pallas_call(
        functools.partial(_fox_prompt_kernel, tq=tq, scale=1.0 / (dh ** 0.5)),
        grid=(batch, heads, nq),
        in_specs=[
            pl.BlockSpec((tq, dh), lambda b, h, i: (b * nq + i, h)),
            pl.BlockSpec((seq, dh), lambda b, h, i: (b, h)),
            pl.BlockSpec((seq, dh), lambda b, h, i: (b, h)),
            pl.BlockSpec((1, seq // LANES, LANES), lambda b, h, i: (b * heads + h, 0, 0)),
        ],
        out_specs=pl.BlockSpec((tq, dh), lambda b, h, i: (b * nq + i, h)),
        out_shape=jax.ShapeDtypeStruct((t, hd), BF16),
        compiler_params=_cparams("parallel", "parallel", "arbitrary"),
        name="fox_prompt",
    )(q, k, v, c_tab)


BIAS_PAGES_PER_STEP = 8
ATTN_PAGES_PER_STEP = 4


def _visit_page(pt, batch_row, visit, n_pages):
    return pt[batch_row * n_pages + (n_pages - 1 - visit)]


def _forget_bias_kernel(*refs, n_sub):
    lfn_ref = refs[1]
    lf_refs = refs[2:2 + n_sub]
    o_ref, run_ref = refs[2 + n_sub:]
    page = lf_refs[0].shape[1]

    @pl.when(pl.program_id(1) == 0)
    def _():
        run_ref[...] = lfn_ref[0]

    later = (_iota((page, page), 1) > _iota((page, page), 0)).astype(F32)
    run = run_ref[...]
    for j in range(n_sub):
        lf = lf_refs[j][0]
        o_ref[0, j] = _dot_exact(later, lf) + run
        run = run + jnp.sum(lf, axis=0, keepdims=True)
    run_ref[...] = run


def forget_bias(logf_new, cache_logf, page_table):
    batch, heads = logf_new.shape
    n_phys, page, _ = cache_logf.shape
    n_pages = page_table.shape[1]
    n_sub = BIAS_PAGES_PER_STEP

    def page_spec(j):
        return pl.BlockSpec((1, page, heads), lambda b, s, pt: (_visit_page(pt, b, s * n_sub + j, n_pages), 0, 0))

    grid_spec = pltpu.PrefetchScalarGridSpec(
        num_scalar_prefetch=1,
        grid=(batch, n_pages // n_sub),
        in_specs=[pl.BlockSpec((1, 1, heads), lambda b, s, pt: (b, 0, 0))] + [page_spec(j) for j in range(n_sub)],
        out_specs=pl.BlockSpec((1, n_sub, page, heads), lambda b, s, pt: (b, s, 0, 0)),
        scratch_shapes=[pltpu.VMEM((1, heads), F32)],
    )
    return pl.pallas_call(
        functools.partial(_forget_bias_kernel, n_sub=n_sub),
        grid_spec=grid_spec,
        out_shape=jax.ShapeDtypeStruct((batch, n_pages, page, heads), F32),
        compiler_params=_cparams("parallel", "arbitrary"),
        name="forget_bias",
    )(page_table.reshape(-1), logf_new.reshape(batch, 1, heads), *([cache_logf] * n_sub))


def _fox_sample_kernel(*refs, n_sub, scale):
    q_ref, kn_ref, vn_ref = refs[1:4]
    page_refs = refs[4:4 + 3 * n_sub]
    o_ref, m_ref, l_ref, acc_ref = refs[4 + 3 * n_sub:]
    step = pl.program_id(1)
    heads, dh = q_ref.shape[1:]
    rows = page_refs[0].shape[1] * heads
    own_head = _iota((heads, rows), 1) % heads == _iota((heads, rows), 0)

    @pl.when(step == 0)
    def _():
        m_ref[...] = jnp.full_like(m_ref, -jnp.inf)
        l_ref[...] = jnp.zeros_like(l_ref)
        acc_ref[...] = jnp.zeros_like(acc_ref)

    q = q_ref[0].astype(BF16)
    m, l, acc = m_ref[...], l_ref[...], acc_ref[...]
    for j in range(n_sub):
        ck_ref, cv_ref, bias_ref = page_refs[3 * j:3 * j + 3]
        k = ck_ref[0].reshape(rows, dh).astype(BF16)
        v = cv_ref[0].reshape(rows, dh).astype(BF16)
        s = lax.dot_general(q, k, NT_DIMS, preferred_element_type=F32) * scale + bias_ref[0, 0]
        s = jnp.where(own_head, s, -jnp.inf)
        m_new = jnp.maximum(m, jnp.max(s, axis=-1, keepdims=True))
        alpha = jnp.exp(m - m_new)
        pr = jnp.exp(s - m_new)
        l = alpha * l + jnp.sum(pr, axis=-1, keepdims=True)
        acc = alpha * acc + _dot(pr.astype(BF16), v)
        m = m_new
    m_ref[...], l_ref[...], acc_ref[...] = m, l, acc

    @pl.when(step == pl.num_programs(1) - 1)
    def _():
        qf = q.astype(F32)
        kn = kn_ref[0].astype(BF16).astype(F32)
        vn = vn_ref[0].astype(BF16).astype(F32)
        s_new = jnp.sum(qf * kn, axis=-1, keepdims=True) * scale
        m_fin = jnp.maximum(m, s_new)
        a_fin = jnp.exp(m - m_fin)
        p_new = jnp.exp(s_new - m_fin)
        l_fin = a_fin * l + p_new
        o_ref[0] = (a_fin * acc + p_new.astype(BF16).astype(F32) * vn) / l_fin


def fox_sample(q, k_new, v_new, logf_new, cache_k, cache_v, cache_logf, page_table):
    batch, heads, dh = q.shape
    page = cache_k.shape[1]
    n_pages = page_table.shape[1]
    n_sub = ATTN_PAGES_PER_STEP
    bias = forget_bias(logf_new, cache_logf, page_table)
    bias = bias.reshape(batch, n_pages, 1, page * heads)

    new_spec = pl.BlockSpec((1, heads, dh), lambda b, s, pt: (b, 0, 0))
    in_specs = [new_spec, new_spec, new_spec]
    operands = [q, k_new, v_new]
    for j in range(n_sub):
        phys = lambda b, s, pt, j=j: (_visit_page(pt, b, s * n_sub + j, n_pages), 0, 0, 0)
        in_specs += [
            pl.BlockSpec((1, page, heads, dh), phys),
            pl.BlockSpec((1, page, heads, dh), phys),
            pl.BlockSpec((1, 1, 1, page * heads), lambda b, s, pt, j=j: (b, s * n_sub + j, 0, 0)),
        ]
        operands += [cache_k, cache_v, bias]
    grid_spec = pltpu.PrefetchScalarGridSpec(
        num_scalar_prefetch=1,
        grid=(batch, n_pages // n_sub),
        in_specs=in_specs,
        out_specs=pl.BlockSpec((1, heads, dh), lambda b, s, pt: (b, 0, 0)),
        scratch_shapes=[pltpu.VMEM((heads, 1), F32), pltpu.VMEM((heads, 1), F32), pltpu.VMEM((heads, dh), F32)],
    )
    return pl.pallas_call(
        functools.partial(_fox_sample_kernel, n_sub=n_sub, scale=1.0 / (dh ** 0.5)),
        grid_spec=grid_spec,
        out_shape=jax.ShapeDtypeStruct((batch, heads, dh), F32),
        compiler_params=_cparams("parallel", "arbitrary"),
        name="fox_sample",
    )(page_table.reshape(-1), *operands)


def _tiles(m):
    return min(m, 512)


def _trunk(y, w, *, mamba_mixer, fox_mixer):
    m, d = y.shape
    tm = _tiles(m)
    nw = w["norm_w"]
    extras = {}

    def ffn(y, layer, half):
        pre, post = nw[layer, 4 * half], nw[layer, 4 * half + 1]
        if (layer, half) in w["ffn_bf16"]:
            w_in, w_out = w["ffn_bf16"][layer, half]
            return ffn_half(y, pre, post, w_in, w_out, None, tm=tm, tf=FFN_TF)
        y, w["ffn_bf16"][layer, half] = ffn_half(y, pre, post, w["ffn_in"], w["ffn_out"], (layer, half),
                                                 tm=tm, tf=FFN_TF)
        return y

    def proj(y, norm, name, cols, out_dtypes=(F32,)):
        key = (name, cols)
        if key in w["proj_bf16"]:
            return norm_matmul(y, norm, w["proj_bf16"][key], tm=tm, tn=PROJ_TN, out_dtypes=out_dtypes)
        *outs, w["proj_bf16"][key] = norm_matmul(y, norm, w[name], tm=tm, tn=PROJ_TN, cols=cols,
                                                 out_dtypes=out_dtypes, emit_bf16=True)
        return outs[0] if len(outs) == 1 else outs

    for layer in range(2):
        y = ffn(y, layer, 0)
        if layer == 0:
            d_inner, conv_dim, nh = w["in_split"]
            z = proj(y, nw[0, 2], "in_proj", (0, d_inner))
            xbc = proj(y, nw[0, 2], "in_proj", (d_inner, conv_dim))
            dt_raw = proj(y, nw[0, 2], "in_proj", (d_inner + conv_dim, nh))
            mixed, extras["conv"], extras["ssm"] = mamba_mixer(z, xbc, dt_raw)
            y = matmul_norm_residual(mixed, w["out_proj"], nw[0, 3], y, tm=tm, tk=PROJ_TK)
        else:
            hd = w["w_q"].shape[1]
            q = proj(y, nw[1, 2], "w_q", (0, hd), out_dtypes=(BF16,))
            attn = fox_mixer(q, extras)
            y = matmul_norm_residual(attn, w["w_o"], nw[1, 3], y, tm=tm, tk=PROJ_TK)
        y = ffn(y, layer, 1)
        if layer == 0:
            hd = w["w_q"].shape[1]
            extras["k"], extras["k_bf"] = proj(y, w["kv_norm_w"], "w_kv", (0, hd), out_dtypes=(F32, BF16))
            extras["v"], extras["v_bf"] = proj(y, w["kv_norm_w"], "w_kv", (hd, hd), out_dtypes=(F32, BF16))
            extras["logf"] = norm_matmul(y, w["kv_norm_w"], w["w_f"], tm=tm, tn=LANES, logsig_bias=w["b_f"])
    return y, extras


def kernel(x_prompt, x_sample, state_conv, state_ssm, cache_k, cache_v, cache_logf, page_table,
           norm_w, ffn_w_in, ffn_w_out, m_in_proj, m_conv_w, m_conv_b, m_dt_bias, m_A_log, m_D,
           m_gnorm_w, m_out_proj, kv_norm_w, w_kvf, b_fg, w_q, w_o):
    bp, seq, d = x_prompt.shape
    bs = x_sample.shape[0]
    heads = cache_k.shape[2]
    dh = cache_k.shape[3]
    hd = heads * dh
    nh = m_dt_bias.shape[1]
    d_inner = nh * SSM_HEAD_DIM
    conv_dim = m_conv_w.shape[2]

    pad_f = LANES - heads
    w = {
        "norm_w": norm_w,
        "ffn_in": ffn_w_in,
        "ffn_out": ffn_w_out,
        "ffn_bf16": {},
        "in_proj": m_in_proj[0],
        "in_split": (d_inner, conv_dim, nh),
        "w_kv": w_kvf,
        "w_q": w_q[0],
        "proj_bf16": {},
        "out_proj": m_out_proj[0].astype(BF16),
        "w_o": w_o[0].astype(BF16),
        "kv_norm_w": kv_norm_w,
        "w_f": jnp.pad(w_kvf[:, 2 * hd:], ((0, 0), (0, pad_f))).astype(BF16),
        "b_f": jnp.pad(b_fg, (0, pad_f)),
    }
    d_exp = jnp.repeat(m_D[0], SSM_HEAD_DIM)
    conv_w, conv_b = m_conv_w[0], m_conv_b[0]

    def mamba_prompt(z, xbc, dt_raw):
        act, tail = conv_prompt(xbc, conv_w, conv_b, batch=bp, tc=512)
        dt, acs = ssd_prep(dt_raw, m_dt_bias[0], m_A_log[0])
        yb, hfin = ssd_prompt(act, z, dt, acs, d_exp, m_gnorm_w[0], batch=bp)
        return yb, tail, hfin.reshape(bp, nh, SSM_HEAD_DIM, D_STATE)

    def fox_prompt_mixer(q, ex):
        logf3 = ex["logf"].reshape(bp, seq, LANES)
        c = cumsum_seq(logf3)[:, :, :heads]
        c_tab = c.transpose(0, 2, 1).reshape(bp * heads, seq // LANES, LANES)
        return fox_prompt(q, ex["k_bf"], ex["v_bf"], c_tab, batch=bp, heads=heads, tq=512)

    def mamba_sample(z, xbc, dt_raw):
        st = state_conv[0].transpose(1, 0, 2)
        act, nst = conv_step(st, xbc, conv_w, conv_b, tc=2048)
        h_new, yg = ssd_step(state_ssm[0], act, z, dt_raw, m_dt_bias[0], m_A_log[0], d_exp, m_gnorm_w[0])
        yg = jnp.pad(yg, ((0, SAMPLE_ROWS - bs), (0, 0))).astype(BF16)
        return yg, nst.transpose(1, 0, 2), h_new

    def fox_sample_mixer(q, ex):
        per_head = lambda a: a[:bs].astype(F32).reshape(bs, heads, dh)
        attn = fox_sample(per_head(q), per_head(ex["k"]), per_head(ex["v"]), ex["logf"][:bs, :heads],
                          cache_k, cache_v, cache_logf, page_table)
        return jnp.pad(attn.reshape(bs, hd), ((0, SAMPLE_ROWS - bs), (0, 0))).astype(BF16)

    xs_rows = jnp.pad(x_sample.reshape(bs, d), ((0, SAMPLE_ROWS - bs), (0, 0)))
    y_s, ex_s = _trunk(xs_rows, w, mamba_mixer=mamba_sample, fox_mixer=fox_sample_mixer)
    y_p, ex_p = _trunk(x_prompt.reshape(bp * seq, d), w, mamba_mixer=mamba_prompt, fox_mixer=fox_prompt_mixer)

    return (
        y_p.reshape(bp, seq, d),
        y_s[:bs].reshape(bs, 1, d),
        ex_p["conv"][None],
        ex_p["ssm"][None],
        ex_p["k"].reshape(bp, seq, heads, dh),
        ex_p["v"].reshape(bp, seq, heads, dh),
        ex_p["logf"][:, :heads].reshape(bp, seq, heads),
        ex_s["conv"][None],
        ex_s["ssm"][None],
        ex_s["k"][:bs].reshape(bs, 1, heads, dh),
        ex_s["v"][:bs].reshape(bs, 1, heads, dh),
        ex_s["logf"][:bs, :heads].reshape(bs, 1, heads),
    )
```

```python
import functools

import jax
import jax.numpy as jnp
from jax import lax
from jax.experimental import pallas as pl
from jax.experimental.pallas import tpu as pltpu

F32 = jnp.float32
BF16 = jnp.bfloat16
HIGHEST = lax.Precision.HIGHEST
EPS = 1e-6

SSM_HEAD_DIM = 64
SSM_GROUPS = 8
D_STATE = 128
CONV_W = 4
CHUNK = 128
ATTN_HEAD_DIM = 128
PAGE_SIZE = 128

LANES = 128
SAMPLE_ROWS = 16
PROJ_TN = 1024
PROJ_TK = 512
FFN_TF = 256
VMEM_LIMIT = 56 * 1024 * 1024

NT_DIMS = (((1,), (1,)), ((), ()))


def _cparams(*semantics):
    return pltpu.CompilerParams(dimension_semantics=semantics, vmem_limit_bytes=VMEM_LIMIT)


def _rms(x, w):
    return x * lax.rsqrt(jnp.mean(x * x, axis=-1, keepdims=True) + EPS) * w


def _silu(x):
    return x * jax.nn.sigmoid(x)


def _softplus(x):
    return jnp.maximum(x, 0.0) + jnp.log1p(jnp.exp(-jnp.abs(x)))


def _dot(a, b):
    return jnp.dot(a, b, preferred_element_type=F32)


def _dot_exact(a, b):
    return jnp.dot(a, b, precision=HIGHEST, preferred_element_type=F32)


def _iota(shape, axis):
    return lax.broadcasted_iota(jnp.int32, shape, axis)


def _col_bcast(row):
    return jnp.broadcast_to(row, (LANES, LANES)).T


def _ffn_kernel(*refs, emit_bf16, has_next):
    x_ref, pre_ref, post_ref, wg_ref, wu_ref, wo_ref = refs[:6]
    n_in = 6 + has_next
    o_ref = refs[n_in]
    w_out_refs = refs[n_in + 1:n_in + 4] if emit_bf16 else ()
    xn_ref = refs[-1]
    f = pl.program_id(1)

    @pl.when(f == 0)
    def _():
        xn_ref[...] = _rms(x_ref[...], pre_ref[...]).astype(BF16)
        o_ref[...] = jnp.zeros_like(o_ref)

    wg, wu, wo = (r[...].astype(BF16) for r in (wg_ref, wu_ref, wo_ref))
    for w_tile, out_ref in zip((wg, wu, wo), w_out_refs):
        out_ref[...] = w_tile
    xn = xn_ref[...]
    h = (_silu(_dot(xn, wg)) * _dot(xn, wu)).astype(BF16)
    o_ref[...] += _dot(h, wo)

    @pl.when(f == pl.num_programs(1) - 1)
    def _():
        y = x_ref[...] + 0.5 * _rms(o_ref[...], post_ref[...])
        o_ref[...] = y
        if has_next:
            xn_ref[...] = _rms(y, refs[6][...]).astype(BF16)


def ffn_half(x, pre_w, post_w, w_in, w_out, which, *, tm, tf, next_norm_w=None):
    m, d = x.shape
    emit = which is not None
    has_next = next_norm_w is not None
    row_vec = pl.BlockSpec((1, d), lambda i, f: (0, 0))
    row_blk = pl.BlockSpec((tm, d), lambda i, f: (i, 0))
    if emit:
        assert m == tm
        lay, half = which
        dff = w_out.shape[2]
        nf = dff // tf
        w_specs = [
            pl.BlockSpec((None, None, d, tf), lambda i, f: (lay, half, 0, f)),
            pl.BlockSpec((None, None, d, tf), lambda i, f: (lay, half, 0, f + nf)),
            pl.BlockSpec((None, None, tf, d), lambda i, f: (lay, half, f, 0)),
        ]
        w_args = (w_in, w_in, w_out)
    else:
        dff = w_out.shape[0]
        w_specs = [
            pl.BlockSpec((d, tf), lambda i, f: (0, f)),
            pl.BlockSpec((d, tf), lambda i, f: (0, f)),
            pl.BlockSpec((tf, d), lambda i, f: (f, 0)),
        ]
        w_args = (*w_in, w_out)
    in_specs = [pl.BlockSpec((tm, d), lambda i, f: (i, 0), pipeline_mode=pl.Buffered(1)), row_vec, row_vec, *w_specs]
    args = [x, pre_w.reshape(1, d), post_w.reshape(1, d), *w_args]
    out_specs = [row_blk]
    out_shape = [jax.ShapeDtypeStruct((m, d), F32)]
    scratch = []
    if has_next:
        in_specs.append(row_vec)
        args.append(next_norm_w.reshape(1, d))
    if emit:
        out_specs += [
            pl.BlockSpec((d, tf), lambda i, f: (0, f)),
            pl.BlockSpec((d, tf), lambda i, f: (0, f)),
            pl.BlockSpec((tf, d), lambda i, f: (f, 0)),
        ]
        out_shape += [jax.ShapeDtypeStruct((d, dff), BF16)] * 2 + [jax.ShapeDtypeStruct((dff, d), BF16)]
    if has_next:
        out_specs.append(row_blk)
        out_shape.append(jax.ShapeDtypeStruct((m, d), BF16))
    else:
        scratch.append(pltpu.VMEM((tm, d), BF16))
    outs = pl.pallas_call(
        functools.partial(_ffn_kernel, emit_bf16=emit, has_next=has_next),
        grid=(m // tm, dff // tf),
        in_specs=in_specs,
        out_specs=out_specs,
        out_shape=out_shape,
        scratch_shapes=scratch,
        compiler_params=_cparams("parallel", "arbitrary"),
        name="ffn_half",
    )(*args)
    result = [outs[0]]
    if emit:
        result.append(((outs[1], outs[2]), outs[3]))
    if has_next:
        result.append(outs[-1])
    return result[0] if len(result) == 1 else tuple(result)


def _norm_matmul_kernel(*refs, has_norm, has_bias, n_out, emit_bf16):
    x_ref, nw_ref, w_ref = refs[:3]
    b_ref = refs[3] if has_bias else None
    o_refs = refs[3 + has_bias:3 + has_bias + n_out]

    if has_norm:
        xn_ref = refs[-1]

        @pl.when(pl.program_id(1) == 0)
        def _():
            xn_ref[...] = _rms(x_ref[...], nw_ref[...]).astype(BF16)
    else:
        xn_ref = x_ref

    w_tile = w_ref[...].astype(BF16)
    if emit_bf16:
        refs[3 + has_bias + n_out][...] = w_tile
    acc = _dot(xn_ref[...], w_tile)
    if has_bias:
        acc = -_softplus(-(acc + b_ref[...]))
    for o_ref in o_refs:
        o_ref[...] = acc.astype(o_ref.dtype)


def norm_matmul(x, norm_w, w, *, tm, tn, cols=None, out_dtypes=(F32,), logsig_bias=None, emit_bf16=False):
    m, d = x.shape
    col0, n = cols if cols is not None else (0, w.shape[1])
    tn = min(tn, n)
    first = col0 // tn
    assert first * tn == col0 and n % tn == 0 and m % tm == 0 and (m == tm or not emit_bf16)
    has_bias = logsig_bias is not None
    has_norm = norm_w is not None
    if not has_norm:
        assert x.dtype == BF16
        norm_w = jnp.ones((d,), F32)
    in_specs = [
        pl.BlockSpec((tm, d), lambda i, j: (i, 0)),
        pl.BlockSpec((1, d), lambda i, j: (0, 0)),
        pl.BlockSpec((d, tn), lambda i, j: (0, first + j)),
    ]
    args = [x, norm_w.reshape(1, d), w]
    if has_bias:
        in_specs.append(pl.BlockSpec((1, tn), lambda i, j: (0, j)))
        args.append(logsig_bias.reshape(1, n))
    out_specs = [pl.BlockSpec((tm, tn), lambda i, j: (i, j)) for _ in out_dtypes]
    out_shape = [jax.ShapeDtypeStruct((m, n), dt) for dt in out_dtypes]
    if emit_bf16:
        out_specs.append(pl.BlockSpec((d, tn), lambda i, j: (0, j)))
        out_shape.append(jax.ShapeDtypeStruct((d, n), BF16))
    outs = pl.pallas_call(
        functools.partial(_norm_matmul_kernel, has_norm=has_norm, has_bias=has_bias, n_out=len(out_dtypes),
                          emit_bf16=emit_bf16),
        grid=(m // tm, n // tn),
        in_specs=in_specs,
        out_specs=out_specs,
        out_shape=out_shape,
        scratch_shapes=[pltpu.VMEM((tm, d), BF16)] if has_norm else [],
        compiler_params=_cparams("parallel", "arbitrary"),
        name="norm_matmul",
    )(*args)
    return outs[0] if len(outs) == 1 else outs


def _matmul_norm_res_kernel(a_ref, w_ref, post_ref, res_ref, o_ref):
    k = pl.program_id(1)

    @pl.when(k == 0)
    def _():
        o_ref[...] = jnp.zeros_like(o_ref)

    o_ref[...] += _dot(a_ref[...], w_ref[...])

    @pl.when(k == pl.num_programs(1) - 1)
    def _():
        o_ref[...] = res_ref[...] + _rms(o_ref[...], post_ref[...])


def matmul_norm_residual(a, w, post_w, res, *, tm, tk):
    m, kdim = a.shape
    n = w.shape[1]
    return pl.pallas_call(
        _matmul_norm_res_kernel,
        grid=(m // tm, kdim // tk),
        in_specs=[
            pl.BlockSpec((tm, tk), lambda i, k: (i, k)),
            pl.BlockSpec((tk, n), lambda i, k: (k, 0)),
            pl.BlockSpec((1, n), lambda i, k: (0, 0)),
            pl.BlockSpec((tm, n), lambda i, k: (i, 0)),
        ],
        out_specs=pl.BlockSpec((tm, n), lambda i, k: (i, 0)),
        out_shape=jax.ShapeDtypeStruct((m, n), F32),
        compiler_params=_cparams("parallel", "arbitrary"),
        name="matmul_norm_residual",
    )(a, w, post_w.reshape(1, n), res)


def _conv_prompt_kernel(x_ref, w_ref, b_ref, o_ref, tail_ref):
    x = x_ref[0]
    seq = x.shape[0]
    row = _iota(x.shape, 0)
    acc = b_ref[...]
    for tap in range(CONV_W):
        back = CONV_W - 1 - tap
        xs = x if back == 0 else jnp.where(row >= back, pltpu.roll(x, back, 0), 0.0)
        acc = acc + xs * w_ref[tap:tap + 1, :]
    o_ref[0] = _silu(acc)
    tail_ref[0] = x[seq - (CONV_W - 1):, :]


def conv_prompt(xbc, conv_w, conv_b, *, batch, tc):
    t, c = xbc.shape
    seq = t // batch
    x3 = xbc.reshape(batch, seq, c)
    act, tail = pl.pallas_call(
        _conv_prompt_kernel,
        grid=(batch, c // tc),
        in_specs=[
            pl.BlockSpec((1, seq, tc), lambda b, j: (b, 0, j)),
            pl.BlockSpec((CONV_W, tc), lambda b, j: (0, j)),
            pl.BlockSpec((1, tc), lambda b, j: (0, j)),
        ],
        out_specs=[
            pl.BlockSpec((1, seq, tc), lambda b, j: (b, 0, j)),
            pl.BlockSpec((1, CONV_W - 1, tc), lambda b, j: (b, 0, j)),
        ],
        out_shape=[
            jax.ShapeDtypeStruct((batch, seq, c), F32),
            jax.ShapeDtypeStruct((batch, CONV_W - 1, c), F32),
        ],
        compiler_params=_cparams("parallel", "parallel"),
        name="conv_prompt",
    )(x3, conv_w, conv_b.reshape(1, c))
    return act.reshape(t, c), tail


def _ssd_prep_kernel(dtr_ref, bias_ref, alog_ref, dt_ref, acs_ref):
    dt = _softplus(dtr_ref[...] + bias_ref[...])
    da = dt * (-jnp.exp(alog_ref[...]))
    lower = (_iota((CHUNK, CHUNK), 1) <= _iota((CHUNK, CHUNK), 0)).astype(F32)
    dt_ref[...] = dt
    acs_ref[...] = _dot_exact(lower, da)


def ssd_prep(dt_raw, dt_bias, a_log):
    t, nh = dt_raw.shape
    blk = pl.BlockSpec((CHUNK, nh), lambda c: (c, 0))
    vec = pl.BlockSpec((1, nh), lambda c: (0, 0))
    return pl.pallas_call(
        _ssd_prep_kernel,
        grid=(t // CHUNK,),
        in_specs=[blk, vec, vec],
        out_specs=[blk, blk],
        out_shape=[jax.ShapeDtypeStruct((t, nh), F32)] * 2,
        compiler_params=_cparams("parallel"),
        name="ssd_prep",
    )(dt_raw, dt_bias.reshape(1, nh), a_log.reshape(1, nh))


def _ssd_kernel(x_ref, b_ref, c_ref, z_ref, acsc_ref, acst_ref, dtt_ref, d_ref, gw_ref,
                y_ref, hfin_ref, ht_ref, yd_ref, *, heads_per_group):
    c = pl.program_id(2)
    p_n = SSM_HEAD_DIM
    pair = LANES // p_n
    assert CHUNK == D_STATE == LANES and heads_per_group % pair == 0

    @pl.when(c == 0)
    def _():
        ht_ref[...] = jnp.zeros_like(ht_ref)

    bm = b_ref[...]
    cm = c_ref[...]
    bm_t = bm.T
    cb = lax.dot_general(cm.astype(BF16), bm.astype(BF16), NT_DIMS, preferred_element_type=F32)
    causal = _iota((CHUNK, CHUNK), 1) <= _iota((CHUNK, CHUNK), 0)
    lane_head = _iota((CHUNK, LANES), 1) // p_n
    acsc = acsc_ref[0, 0]
    acst = acst_ref[0, 0]
    dtt = dtt_ref[0, 0]

    for slab in range(heads_per_group // pair):
        lanes = slice(slab * LANES, (slab + 1) * LANES)
        x_bf = x_ref[:, lanes].astype(BF16)
        ht = ht_ref[:, lanes]
        rhs = jnp.concatenate([x_bf, ht.astype(BF16)], axis=0)
        y_slab = s_slab = keep = None
        for k in range(pair):
            r = slab * pair + k
            a_col = jnp.broadcast_to(acsc[:, r:r + 1], (CHUNK, CHUNK))
            a_row = acst[r:r + 1, :]
            dt_row = dtt[r:r + 1, :]
            decay = jnp.exp(jnp.where(causal, a_col - a_row, -jnp.inf))
            within = cb * decay * dt_row
            carried = cm * jnp.exp(a_col)
            lhs = jnp.concatenate([within, carried], axis=1).astype(BF16)
            y_r = _dot(lhs, rhs)
            a_last = a_row[:, CHUNK - 1:CHUNK]
            to_end = jnp.exp(a_last - a_row) * dt_row
            s_r = _dot((bm_t * to_end).astype(BF16), x_bf)
            k_r = jnp.broadcast_to(jnp.exp(a_last), (D_STATE, LANES))
            if k == 0:
                y_slab, s_slab, keep = y_r, s_r, k_r
            else:
                mine = lane_head == k
                y_slab = jnp.where(mine, y_r, y_slab)
                s_slab = jnp.where(mine, s_r, s_slab)
                keep = jnp.where(mine, k_r, keep)
        yd_ref[:, lanes] = y_slab
        ht_ref[:, lanes] = ht * keep + s_slab

    y = yd_ref[...] + d_ref[...] * x_ref[...]
    y = y * _silu(z_ref[...])
    y_ref[...] = _rms(y, gw_ref[...]).astype(y_ref.dtype)

    @pl.when(c == pl.num_programs(2) - 1)
    def _():
        hfin_ref[0] = ht_ref[...].T


def ssd_prompt(xbc_act, z, dt, acs, d_exp, gnorm_w, *, batch):
    t, conv_dim = xbc_act.shape
    d_inner = z.shape[1]
    nh = dt.shape[1]
    groups = SSM_GROUPS
    r_n = nh // groups
    gp = d_inner // groups
    nc = t // batch // CHUNK
    n_chunks = t // CHUNK
    per_group = lambda a: a.reshape(n_chunks, CHUNK, groups, r_n).transpose(0, 2, 1, 3)
    acs_g = per_group(acs)
    acs_gt = acs_g.transpose(0, 1, 3, 2)
    dt_gt = per_group(dt).transpose(0, 1, 3, 2)
    xoff = d_inner // D_STATE
    row = lambda b, g, c: b * nc + c
    col_form = pl.BlockSpec((1, 1, CHUNK, r_n), lambda b, g, c: (row(b, g, c), g, 0, 0))
    row_form = pl.BlockSpec((1, 1, r_n, CHUNK), lambda b, g, c: (row(b, g, c), g, 0, 0))
    y, hfin = pl.pallas_call(
        functools.partial(_ssd_kernel, heads_per_group=r_n),
        grid=(batch, groups, nc),
        in_specs=[
            pl.BlockSpec((CHUNK, gp), lambda b, g, c: (row(b, g, c), g)),
            pl.BlockSpec((CHUNK, D_STATE), lambda b, g, c: (row(b, g, c), xoff + g)),
            pl.BlockSpec((CHUNK, D_STATE), lambda b, g, c: (row(b, g, c), xoff + groups + g)),
            pl.BlockSpec((CHUNK, gp), lambda b, g, c: (row(b, g, c), g)),
            col_form, row_form, row_form,
            pl.BlockSpec((1, gp), lambda b, g, c: (0, g)),
            pl.BlockSpec((1, gp), lambda b, g, c: (0, g)),
        ],
        out_specs=[
            pl.BlockSpec((CHUNK, gp), lambda b, g, c: (row(b, g, c), g)),
            pl.BlockSpec((1, gp, D_STATE), lambda b, g, c: (b, g, 0)),
        ],
        out_shape=[
            jax.ShapeDtypeStruct((t, d_inner), BF16),
            jax.ShapeDtypeStruct((batch, d_inner, D_STATE), F32),
        ],
        scratch_shapes=[pltpu.VMEM((D_STATE, gp), F32), pltpu.VMEM((CHUNK, gp), F32)],
        compiler_params=_cparams("parallel", "parallel", "arbitrary"),
        name="ssd_prompt",
    )(xbc_act, xbc_act, xbc_act, z, acs_g, acs_gt, dt_gt, d_exp.reshape(1, d_inner),
      gnorm_w.reshape(1, d_inner))
    return y, hfin


def _conv_step_kernel(st_ref, new_ref, w_ref, b_ref, act_ref, nst_ref):
    xn = new_ref[...]
    acc = b_ref[...]
    for tap in range(CONV_W - 1):
        acc = acc + st_ref[tap] * w_ref[tap:tap + 1, :]
        if tap > 0:
            nst_ref[tap - 1] = st_ref[tap]
    acc = acc + xn * w_ref[CONV_W - 1:CONV_W, :]
    nst_ref[CONV_W - 2] = xn
    act_ref[...] = _silu(acc)


def conv_step(state_t, xbc_new, conv_w, conv_b, *, tc):
    taps, batch, c = state_t.shape
    return pl.pallas_call(
        _conv_step_kernel,
        grid=(c // tc,),
        in_specs=[
            pl.BlockSpec((taps, batch, tc), lambda j: (0, 0, j)),
            pl.BlockSpec((batch, tc), lambda j: (0, j)),
            pl.BlockSpec((CONV_W, tc), lambda j: (0, j)),
            pl.BlockSpec((1, tc), lambda j: (0, j)),
        ],
        out_specs=[
            pl.BlockSpec((batch, tc), lambda j: (0, j)),
            pl.BlockSpec((taps, batch, tc), lambda j: (0, 0, j)),
        ],
        out_shape=[
            jax.ShapeDtypeStruct((batch, c), F32),
            jax.ShapeDtypeStruct((taps, batch, c), F32),
        ],
        compiler_params=_cparams("parallel"),
        name="conv_step",
    )(state_t, xbc_new, conv_w, conv_b.reshape(1, c))


def _ssd_step_kernel(h_ref, xs_ref, b_ref, c_ref, z_ref, dtr_ref, bias_ref, alog_ref, d_ref, gw_ref,
                     ho_ref, y_ref, dte_ref, dece_ref, *, heads_per_group):
    g = pl.program_id(0)
    r_n, p_n = heads_per_group, SSM_HEAD_DIM
    gp = r_n * p_n
    batch, nh = dtr_ref.shape
    rows_per_tile = LANES // p_n

    expand = (_iota((nh, gp), 1) // p_n + g * r_n == _iota((nh, gp), 0)).astype(F32)
    dt = _softplus(dtr_ref[...] + bias_ref[...])
    dte_ref[...] = _dot_exact(dt, expand)
    dece_ref[...] = _dot_exact(jnp.exp(dt * (-jnp.exp(alog_ref[...]))), expand)

    def body(b, carry):
        x = xs_ref[pl.ds(b, 1), :]
        bv = b_ref[pl.ds(b, 1), :]
        cv = c_ref[pl.ds(b, 1), :]
        xdt = x * dte_ref[pl.ds(b, 1), :]
        dec = dece_ref[pl.ds(b, 1), :]
        y_parts = []
        for j in range(gp // LANES):
            lanes = slice(j * LANES, (j + 1) * LANES)
            heads = pl.ds(j * rows_per_tile, rows_per_tile)
            h = h_ref[b, heads].reshape(LANES, D_STATE)
            h_new = h * _col_bcast(dec[:, lanes]) + _col_bcast(xdt[:, lanes]) * bv
            ho_ref[b, heads] = h_new.reshape(rows_per_tile, p_n, D_STATE)
            ycol = jnp.sum(h_new * cv, axis=-1, keepdims=True)
            y_parts.append(jnp.broadcast_to(ycol, (LANES, LANES)).T[0:1, :])
        y = jnp.concatenate(y_parts, axis=1) + d_ref[...] * x
        y = y * _silu(z_ref[pl.ds(b, 1), :])
        y_ref[pl.ds(b, 1), :] = _rms(y, gw_ref[...])
        return carry

    lax.fori_loop(0, batch, body, 0)


def ssd_step(h, xbc_act, z, dt_raw, dt_bias, a_log, d_exp, gnorm_w):
    batch, nh, p_n, n_n = h.shape
    d_inner = nh * p_n
    groups = SSM_GROUPS
    r_n = nh // groups
    gp = d_inner // groups
    xoff = d_inner // D_STATE
    return pl.pallas_call(
        functools.partial(_ssd_step_kernel, heads_per_group=r_n),
        grid=(groups,),
        in_specs=[
            pl.BlockSpec((batch, r_n, p_n, n_n), lambda g: (0, g, 0, 0)),
            pl.BlockSpec((batch, gp), lambda g: (0, g)),
            pl.BlockSpec((batch, D_STATE), lambda g: (0, xoff + g)),
            pl.BlockSpec((batch, D_STATE), lambda g: (0, xoff + groups + g)),
            pl.BlockSpec((batch, gp), lambda g: (0, g)),
            pl.BlockSpec((batch, nh), lambda g: (0, 0)),
            pl.BlockSpec((1, nh), lambda g: (0, 0)),
            pl.BlockSpec((1, nh), lambda g: (0, 0)),
            pl.BlockSpec((1, gp), lambda g: (0, g)),
            pl.BlockSpec((1, gp), lambda g: (0, g)),
        ],
        out_specs=[
            pl.BlockSpec((batch, r_n, p_n, n_n), lambda g: (0, g, 0, 0)),
            pl.BlockSpec((batch, gp), lambda g: (0, g)),
        ],
        out_shape=[
            jax.ShapeDtypeStruct(h.shape, F32),
            jax.ShapeDtypeStruct((batch, d_inner), F32),
        ],
        scratch_shapes=[pltpu.VMEM((batch, gp), F32), pltpu.VMEM((batch, gp), F32)],
        compiler_params=_cparams("parallel"),
        name="ssd_step",
    )(h, xbc_act, xbc_act, xbc_act, z, dt_raw, dt_bias.reshape(1, nh), a_log.reshape(1, nh),
      d_exp.reshape(1, d_inner), gnorm_w.reshape(1, d_inner))


def _cumsum_kernel(x_ref, o_ref, *, tile):
    seq = x_ref.shape[1]
    lower = (_iota((tile, tile), 1) <= _iota((tile, tile), 0)).astype(F32)
    carry = jnp.zeros((1, x_ref.shape[2]), F32)
    for i in range(seq // tile):
        cs = _dot_exact(lower, x_ref[0, i * tile:(i + 1) * tile, :]) + carry
        o_ref[0, i * tile:(i + 1) * tile, :] = cs
        carry = cs[tile - 1:tile, :]


def cumsum_seq(x3, *, tile=256):
    batch, seq, n = x3.shape
    blk = pl.BlockSpec((1, seq, n), lambda b: (b, 0, 0))
    return pl.pallas_call(
        functools.partial(_cumsum_kernel, tile=tile),
        grid=(batch,),
        in_specs=[blk],
        out_specs=blk,
        out_shape=jax.ShapeDtypeStruct(x3.shape, F32),
        compiler_params=_cparams("parallel"),
        name="cumsum_seq",
    )(x3)


def _fox_prompt_kernel(q_ref, k_ref, v_ref, c_ref, o_ref, *, tq, scale):
    q = q_ref[...]
    sub = tq // LANES
    n_q = k_ref.shape[0] // tq

    def block(ki, carry, cq, masked):
        m, l, acc = carry
        k = k_ref[ki * tq:(ki + 1) * tq, :]
        v = v_ref[ki * tq:(ki + 1) * tq, :]
        ck_rows = c_ref[0, ki * sub:(ki + 1) * sub, :]
        ck = jnp.concatenate([jnp.broadcast_to(ck_rows[j:j + 1, :], (tq, LANES)) for j in range(sub)], axis=1)
        s = lax.dot_general(q, k, NT_DIMS, preferred_element_type=F32) * scale
        s = s + cq - ck
        if masked:
            s = jnp.where(_iota((tq, tq), 1) <= _iota((tq, tq), 0), s, -jnp.inf)
        m_new = jnp.maximum(m, jnp.max(s, axis=-1, keepdims=True))
        alpha = jnp.exp(m - m_new)
        p = jnp.exp(s - m_new)
        l = alpha * l + jnp.sum(p, axis=-1, keepdims=True)
        acc = alpha * acc + _dot(p.astype(BF16), v)
        return m_new, l, acc

    for n_before in range(n_q):
        @pl.when(pl.program_id(2) == n_before)
        def _(n_before=n_before):
            cq_rows = c_ref[0, n_before * sub:(n_before + 1) * sub, :]
            cq = jnp.concatenate([_col_bcast(cq_rows[i:i + 1, :]) for i in range(sub)], axis=0)
            cq = jnp.concatenate([cq] * sub, axis=1)
            carry = (jnp.full((tq, 1), -jnp.inf, F32), jnp.zeros((tq, 1), F32), jnp.zeros((tq, q.shape[1]), F32))
            for ki in range(n_before):
                carry = block(ki, carry, cq, False)
            _, l, acc = block(n_before, carry, cq, True)
            o_ref[...] = (acc / l).astype(o_ref.dtype)


def fox_prompt(q, k, v, c_tab, *, batch, heads, tq):
    t, hd = q.shape
    dh = hd // heads
    seq = t // batch
    nq = seq // tq
    return pl.pallas_call(
        functools.partial(_fox_prompt_kernel, tq=tq, scale=1.0 / (dh ** 0.5)),
        grid=(batch, heads, nq),
        in_specs=[
            pl.BlockSpec((tq, dh), lambda b, h, i: (b * nq + i, h)),
            pl.BlockSpec((seq, dh), lambda b, h, i: (b, h)),
            pl.BlockSpec((seq, dh), lambda b, h, i: (b, h)),
            pl.BlockSpec((1, seq // LANES, LANES), lambda b, h, i: (b * heads + h, 0, 0)),
        ],
        out_specs=pl.BlockSpec((tq, dh), lambda b, h, i: (b * nq + i, h)),
        out_shape=jax.ShapeDtypeStruct((t, hd), BF16),
        compiler_params=_cparams("parallel", "parallel", "arbitrary"),
        name="fox_prompt",
    )(q, k, v, c_tab)


BIAS_PAGES_PER_STEP = 8
ATTN_PAGES_PER_STEP = 4


def _visit_page(pt, batch_row, visit, n_pages):
    return pt[batch_row * n_pages + (n_pages - 1 - visit)]


def _forget_bias_kernel(*refs, n_sub):
    lfn_ref = refs[1]
    lf_refs = refs[2:2 + n_sub]
    o_ref, run_ref = refs[2 + n_sub:]
    page = lf_refs[0].shape[1]

    @pl.when(pl.program_id(1) == 0)
    def _():
        run_ref[...] = lfn_ref[0]

    later = (_iota((page, page), 1) > _iota((page, page), 0)).astype(F32)
    run = run_ref[...]
    for j in range(n_sub):
        lf = lf_refs[j][0]
        o_ref[0, j] = _dot_exact(later, lf) + run
        run = run + jnp.sum(lf, axis=0, keepdims=True)
    run_ref[...] = run


def forget_bias(logf_new, cache_logf, page_table):
    batch, heads = logf_new.shape
    n_phys, page, _ = cache_logf.shape
    n_pages = page_table.shape[1]
    n_sub = BIAS_PAGES_PER_STEP

    def page_spec(j):
        return pl.BlockSpec((1, page, heads), lambda b, s, pt: (_visit_page(pt, b, s * n_sub + j, n_pages), 0, 0))

    grid_spec = pltpu.PrefetchScalarGridSpec(
        num_scalar_prefetch=1,
        grid=(batch, n_pages // n_sub),
        in_specs=[pl.BlockSpec((1, 1, heads), lambda b, s, pt: (b, 0, 0))] + [page_spec(j) for j in range(n_sub)],
        out_specs=pl.BlockSpec((1, n_sub, page, heads), lambda b, s, pt: (b, s, 0, 0)),
        scratch_shapes=[pltpu.VMEM((1, heads), F32)],
    )
    return pl.pallas_call(
        functools.partial(_forget_bias_kernel, n_sub=n_sub),
        grid_spec=grid_spec,
        out_shape=jax.ShapeDtypeStruct((batch, n_pages, page, heads), F32),
        compiler_params=_cparams("parallel", "arbitrary"),
        name="forget_bias",
    )(page_table.reshape(-1), logf_new.reshape(batch, 1, heads), *([cache_logf] * n_sub))


def _fox_sample_kernel(*refs, n_sub, scale):
    q_ref, kn_ref, vn_ref = refs[1:4]
    page_refs = refs[4:4 + 3 * n_sub]
    o_ref, m_ref, l_ref, acc_ref = refs[4 + 3 * n_sub:]
    step = pl.program_id(1)
    heads, dh = q_ref.shape[1:]
    rows = page_refs[0].shape[1] * heads
    own_head = _iota((heads, rows), 1) % heads == _iota((heads, rows), 0)

    @pl.when(step == 0)
    def _():
        m_ref[...] = jnp.full_like(m_ref, -jnp.inf)
        l_ref[...] = jnp.zeros_like(l_ref)
        acc_ref[...] = jnp.zeros_like(acc_ref)

    q = q_ref[0].astype(BF16)
    m, l, acc = m_ref[...], l_ref[...], acc_ref[...]
    for j in range(n_sub):
        ck_ref, cv_ref, bias_ref = page_refs[3 * j:3 * j + 3]
        k = ck_ref[0].reshape(rows, dh).astype(BF16)
        v = cv_ref[0].reshape(rows, dh).astype(BF16)
        s = lax.dot_general(q, k, NT_DIMS, preferred_element_type=F32) * scale + bias_ref[0, 0]
        s = jnp.where(own_head, s, -jnp.inf)
        m_new = jnp.maximum(m, jnp.max(s, axis=-1, keepdims=True))
        alpha = jnp.exp(m - m_new)
        pr = jnp.exp(s - m_new)
        l = alpha * l + jnp.sum(pr, axis=-1, keepdims=True)
        acc = alpha * acc + _dot(pr.astype(BF16), v)
        m = m_new
    m_ref[...], l_ref[...], acc_ref[...] = m, l, acc

    @pl.when(step == pl.num_programs(1) - 1)
    def _():
        qf = q.astype(F32)
        kn = kn_ref[0].astype(BF16).astype(F32)
        vn = vn_ref[0].astype(BF16).astype(F32)
        s_new = jnp.sum(qf * kn, axis=-1, keepdims=True) * scale
        m_fin = jnp.maximum(m, s_new)
        a_fin = jnp.exp(m - m_fin)
        p_new = jnp.exp(s_new - m_fin)
        l_fin = a_fin * l + p_new
        o_ref[0] = (a_fin * acc + p_new.astype(BF16).astype(F32) * vn) / l_fin


def fox_sample(q, k_new, v_new, logf_new, cache_k, cache_v, cache_logf, page_table):
    batch, heads, dh = q.shape
    page = cache_k.shape[1]
    n_pages = page_table.shape[1]
    n_sub = ATTN_PAGES_PER_STEP
    bias = forget_bias(logf_new, cache_logf, page_table)
    bias = bias.reshape(batch, n_pages, 1, page * heads)

    new_spec = pl.BlockSpec((1, heads, dh), lambda b, s, pt: (b, 0, 0))
    in_specs = [new_spec, new_spec, new_spec]
    operands = [q, k_new, v_new]
    for j in range(n_sub):
        phys = lambda b, s, pt, j=j: (_visit_page(pt, b, s * n_sub + j, n_pages), 0, 0, 0)
        in_specs += [
            pl.BlockSpec((1, page, heads, dh), phys),
            pl.BlockSpec((1, page, heads, dh), phys),
            pl.BlockSpec((1, 1, 1, page * heads), lambda b, s, pt, j=j: (b, s * n_sub + j, 0, 0)),
        ]
        operands += [cache_k, cache_v, bias]
    grid_spec = pltpu.PrefetchScalarGridSpec(
        num_scalar_prefetch=1,
        grid=(batch, n_pages // n_sub),
        in_specs=in_specs,
        out_specs=pl.BlockSpec((1, heads, dh), lambda b, s, pt: (b, 0, 0)),
        scratch_shapes=[pltpu.VMEM((heads, 1), F32), pltpu.VMEM((heads, 1), F32), pltpu.VMEM((heads, dh), F32)],
    )
    return pl.pallas_call(
        functools.partial(_fox_sample_kernel, n_sub=n_sub, scale=1.0 / (dh ** 0.5)),
        grid_spec=grid_spec,
        out_shape=jax.ShapeDtypeStruct((batch, heads, dh), F32),
        compiler_params=_cparams("parallel", "arbitrary"),
        name="fox_sample",
    )(page_table.reshape(-1), *operands)


def _tiles(m):
    return min(m, 512)


def _trunk(y, w, *, mamba_mixer, fox_mixer):
    m, d = y.shape
    tm = _tiles(m)
    nw = w["norm_w"]
    extras = {}

    def ffn(y, layer, half, next_norm_w=None):
        pre, post = nw[layer, 4 * half], nw[layer, 4 * half + 1]
        kw = dict(tm=tm, tf=FFN_TF, next_norm_w=next_norm_w)
        if (layer, half) in w["ffn_bf16"]:
            w_in, w_out = w["ffn_bf16"][layer, half]
            return ffn_half(y, pre, post, w_in, w_out, None, **kw)
        y, w["ffn_bf16"][layer, half], *rest = ffn_half(y, pre, post, w["ffn_in"], w["ffn_out"], (layer, half), **kw)
        return (y, *rest) if rest else y

    def proj(xn, name, cols, out_dtypes=(F32,)):
        key = (name, cols)
        if key in w["proj_bf16"]:
            return norm_matmul(xn, None, w["proj_bf16"][key], tm=tm, tn=PROJ_TN, out_dtypes=out_dtypes)
        *outs, w["proj_bf16"][key] = norm_matmul(xn, None, w[name], tm=tm, tn=PROJ_TN, cols=cols,
                                                 out_dtypes=out_dtypes, emit_bf16=True)
        return outs[0] if len(outs) == 1 else outs

    hd = w["w_q"].shape[1]
    d_inner, conv_dim, nh = w["in_split"]
    y, xn = ffn(y, 0, 0, next_norm_w=nw[0, 2])
    z = proj(xn, "in_proj", (0, d_inner))
    xbc = proj(xn, "in_proj", (d_inner, conv_dim))
    dt_raw = proj(xn, "in_proj", (d_inner + conv_dim, nh))
    mixed, extras["conv"], extras["ssm"] = mamba_mixer(z, xbc, dt_raw)
    y = matmul_norm_residual(mixed, w["out_proj"], nw[0, 3], y, tm=tm, tk=PROJ_TK)
    y, xn = ffn(y, 0, 1, next_norm_w=w["kv_norm_w"])
    extras["k"], extras["k_bf"] = proj(xn, "w_kv", (0, hd), out_dtypes=(F32, BF16))
    extras["v"], extras["v_bf"] = proj(xn, "w_kv", (hd, hd), out_dtypes=(F32, BF16))
    extras["logf"] = norm_matmul(xn, None, w["w_f"], tm=tm, tn=LANES, logsig_bias=w["b_f"])
    y, xn = ffn(y, 1, 0, next_norm_w=nw[1, 2])
    q = proj(xn, "w_q", (0, hd), out_dtypes=(BF16,))
    attn = fox_mixer(q, extras)
    y = matmul_norm_residual(attn, w["w_o"], nw[1, 3], y, tm=tm, tk=PROJ_TK)
    y = ffn(y, 1, 1)
    return y, extras


def kernel(x_prompt, x_sample, state_conv, state_ssm, cache_k, cache_v, cache_logf, page_table,
           norm_w, ffn_w_in, ffn_w_out, m_in_proj, m_conv_w, m_conv_b, m_dt_bias, m_A_log, m_D,
           m_gnorm_w, m_out_proj, kv_norm_w, w_kvf, b_fg, w_q, w_o):
    bp, seq, d = x_prompt.shape
    bs = x_sample.shape[0]
    heads = cache_k.shape[2]
    dh = cache_k.shape[3]
    hd = heads * dh
    nh = m_dt_bias.shape[1]
    d_inner = nh * SSM_HEAD_DIM
    conv_dim = m_conv_w.shape[2]

    pad_f = LANES - heads
    w = {
        "norm_w": norm_w,
        "ffn_in": ffn_w_in,
        "ffn_out": ffn_w_out,
        "ffn_bf16": {},
        "in_proj": m_in_proj[0],
        "in_split": (d_inner, conv_dim, nh),
        "w_kv": w_kvf,
        "w_q": w_q[0],
        "proj_bf16": {},
        "out_proj": m_out_proj[0].astype(BF16),
        "w_o": w_o[0].astype(BF16),
        "kv_norm_w": kv_norm_w,
        "w_f": jnp.pad(w_kvf[:, 2 * hd:], ((0, 0), (0, pad_f))).astype(BF16),
        "b_f": jnp.pad(b_fg, (0, pad_f)),
    }
    d_exp = jnp.repeat(m_D[0], SSM_HEAD_DIM)
    conv_w, conv_b = m_conv_w[0], m_conv_b[0]

    def mamba_prompt(z, xbc, dt_raw):
        act, tail = conv_prompt(xbc, conv_w, conv_b, batch=bp, tc=512)
        dt, acs = ssd_prep(dt_raw, m_dt_bias[0], m_A_log[0])
        yb, hfin = ssd_prompt(act, z, dt, acs, d_exp, m_gnorm_w[0], batch=bp)
        return yb, tail, hfin.reshape(bp, nh, SSM_HEAD_DIM, D_STATE)

    def fox_prompt_mixer(q, ex):
        logf3 = ex["logf"].reshape(bp, seq, LANES)
        c = cumsum_seq(logf3)[:, :, :heads]
        c_tab = c.transpose(0, 2, 1).reshape(bp * heads, seq // LANES, LANES)
        return fox_prompt(q, ex["k_bf"], ex["v_bf"], c_tab, batch=bp, heads=heads, tq=512)

    def mamba_sample(z, xbc, dt_raw):
        st = state_conv[0].transpose(1, 0, 2)
        act, nst = conv_step(st, xbc, conv_w, conv_b, tc=2048)
        h_new, yg = ssd_step(state_ssm[0], act, z, dt_raw, m_dt_bias[0], m_A_log[0], d_exp, m_gnorm_w[0])
        yg = jnp.pad(yg, ((0, SAMPLE_ROWS - bs), (0, 0))).astype(BF16)
        return yg, nst.transpose(1, 0, 2), h_new

    def fox_sample_mixer(q, ex):
        per_head = lambda a: a[:bs].astype(F32).reshape(bs, heads, dh)
        attn = fox_sample(per_head(q), per_head(ex["k"]), per_head(ex["v"]), ex["logf"][:bs, :heads],
                          cache_k, cache_v, cache_logf, page_table)
        return jnp.pad(attn.reshape(bs, hd), ((0, SAMPLE_ROWS - bs), (0, 0))).astype(BF16)

    xs_rows = jnp.pad(x_sample.reshape(bs, d), ((0, SAMPLE_ROWS - bs), (0, 0)))
    y_s, ex_s = _trunk(xs_rows, w, mamba_mixer=mamba_sample, fox_mixer=fox_sample_mixer)
    y_p, ex_p = _trunk(x_prompt.reshape(bp * seq, d), w, mamba_mixer=mamba_prompt, fox_mixer=fox_prompt_mixer)

    return (
        y_p.reshape(bp, seq, d),
        y_s[:bs].reshape(bs, 1, d),
        ex_p["conv"][None],
        ex_p["ssm"][None],
        ex_p["k"].reshape(bp, seq, heads, dh),
        ex_p["v"].reshape(bp, seq, heads, dh),
        ex_p["logf"][:, :heads].reshape(bp, seq, heads),
        ex_s["conv"][None],
        ex_s["ssm"][None],
        ex_s["k"][:bs].reshape(bs, 1, heads, dh),
        ex_s["v"][:bs].reshape(bs, 1, heads, dh),
        ex_s["logf"][:bs, :heads].reshape(bs, 1, heads),
    )
```

```python
import functools

import jax
import jax.numpy as jnp
from jax import lax
from jax.experimental import pallas as pl
from jax.experimental.pallas import tpu as pltpu

F32 = jnp.float32
BF16 = jnp.bfloat16
HIGHEST = lax.Precision.HIGHEST
EPS = 1e-6
LOG2E = 1.4426950408889634

SSM_HEAD_DIM = 64
SSM_GROUPS = 8
D_STATE = 128
CONV_W = 4
CHUNK = 128
ATTN_HEAD_DIM = 128
PAGE_SIZE = 128

LANES = 128
SAMPLE_ROWS = 16
PROJ_TN = 1024
PROJ_TK = 512
FFN_TF = 256
VMEM_LIMIT = 56 * 1024 * 1024

NT_DIMS = (((1,), (1,)), ((), ()))


def _cparams(*semantics):
    return pltpu.CompilerParams(dimension_semantics=semantics, vmem_limit_bytes=VMEM_LIMIT)


def _rms(x, w):
    return x * lax.rsqrt(jnp.mean(x * x, axis=-1, keepdims=True) + EPS) * w


def _silu(x):
    return x * jax.nn.sigmoid(x)


def _softplus(x):
    return jnp.maximum(x, 0.0) + jnp.log1p(jnp.exp(-jnp.abs(x)))


def _dot(a, b):
    return jnp.dot(a, b, preferred_element_type=F32)


def _dot_exact(a, b):
    return jnp.dot(a, b, precision=HIGHEST, preferred_element_type=F32)


def _iota(shape, axis):
    return lax.broadcasted_iota(jnp.int32, shape, axis)


def _col_bcast(row):
    return jnp.broadcast_to(row, (LANES, LANES)).T


def _ffn_kernel(*refs, emit_bf16, has_next):
    x_ref, pre_ref, post_ref, wg_ref, wu_ref, wo_ref = refs[:6]
    n_in = 6 + has_next
    o_ref = refs[n_in]
    w_out_refs = refs[n_in + 1:n_in + 4] if emit_bf16 else ()
    xn_ref = refs[-1]
    f = pl.program_id(1)

    @pl.when(f == 0)
    def _():
        xn_ref[...] = _rms(x_ref[...], pre_ref[...]).astype(BF16)
        o_ref[...] = jnp.zeros_like(o_ref)

    wg, wu, wo = (r[...].astype(BF16) for r in (wg_ref, wu_ref, wo_ref))
    for w_tile, out_ref in zip((wg, wu, wo), w_out_refs):
        out_ref[...] = w_tile
    xn = xn_ref[...]
    h = (_silu(_dot(xn, wg)) * _dot(xn, wu)).astype(BF16)
    o_ref[...] += _dot(h, wo)

    @pl.when(f == pl.num_programs(1) - 1)
    def _():
        y = x_ref[...] + 0.5 * _rms(o_ref[...], post_ref[...])
        o_ref[...] = y
        if has_next:
            xn_ref[...] = _rms(y, refs[6][...]).astype(BF16)


def ffn_half(x, pre_w, post_w, w_in, w_out, which, *, tm, tf, next_norm_w=None):
    m, d = x.shape
    emit = which is not None
    has_next = next_norm_w is not None
    row_vec = pl.BlockSpec((1, d), lambda i, f: (0, 0))
    row_blk = pl.BlockSpec((tm, d), lambda i, f: (i, 0))
    if emit:
        assert m == tm
        lay, half = which
        dff = w_out.shape[2]
        nf = dff // tf
        w_specs = [
            pl.BlockSpec((None, None, d, tf), lambda i, f: (lay, half, 0, f)),
            pl.BlockSpec((None, None, d, tf), lambda i, f: (lay, half, 0, f + nf)),
            pl.BlockSpec((None, None, tf, d), lambda i, f: (lay, half, f, 0)),
        ]
        w_args = (w_in, w_in, w_out)
    else:
        dff = w_out.shape[0]
        w_specs = [
            pl.BlockSpec((d, tf), lambda i, f: (0, f)),
            pl.BlockSpec((d, tf), lambda i, f: (0, f)),
            pl.BlockSpec((tf, d), lambda i, f: (f, 0)),
        ]
        w_args = (*w_in, w_out)
    in_specs = [pl.BlockSpec((tm, d), lambda i, f: (i, 0), pipeline_mode=pl.Buffered(1)), row_vec, row_vec, *w_specs]
    args = [x, pre_w.reshape(1, d), post_w.reshape(1, d), *w_args]
    out_specs = [row_blk]
    out_shape = [jax.ShapeDtypeStruct((m, d), F32)]
    scratch = []
    if has_next:
        in_specs.append(row_vec)
        args.append(next_norm_w.reshape(1, d))
    if emit:
        out_specs += [
            pl.BlockSpec((d, tf), lambda i, f: (0, f)),
            pl.BlockSpec((d, tf), lambda i, f: (0, f)),
            pl.BlockSpec((tf, d), lambda i, f: (f, 0)),
        ]
        out_shape += [jax.ShapeDtypeStruct((d, dff), BF16)] * 2 + [jax.ShapeDtypeStruct((dff, d), BF16)]
    if has_next:
        out_specs.append(row_blk)
        out_shape.append(jax.ShapeDtypeStruct((m, d), BF16))
    else:
        scratch.append(pltpu.VMEM((tm, d), BF16))
    outs = pl.pallas_call(
        functools.partial(_ffn_kernel, emit_bf16=emit, has_next=has_next),
        grid=(m // tm, dff // tf),
        in_specs=in_specs,
        out_specs=out_specs,
        out_shape=out_shape,
        scratch_shapes=scratch,
        compiler_params=_cparams("parallel", "arbitrary"),
        name="ffn_half",
    )(*args)
    result = [outs[0]]
    if emit:
        result.append(((outs[1], outs[2]), outs[3]))
    if has_next:
        result.append(outs[-1])
    return result[0] if len(result) == 1 else tuple(result)


def _norm_matmul_kernel(*refs, has_norm, has_bias, n_out, emit_bf16, out_scale):
    x_ref, nw_ref, w_ref = refs[:3]
    b_ref = refs[3] if has_bias else None
    o_refs = refs[3 + has_bias:3 + has_bias + n_out]

    if has_norm:
        xn_ref = refs[-1]

        @pl.when(pl.program_id(1) == 0)
        def _():
            xn_ref[...] = _rms(x_ref[...], nw_ref[...]).astype(BF16)
    else:
        xn_ref = x_ref

    w_tile = w_ref[...].astype(BF16)
    if emit_bf16:
        refs[3 + has_bias + n_out][...] = w_tile
    acc = _dot(xn_ref[...], w_tile)
    if out_scale is not None:
        acc = acc * out_scale
    if has_bias:
        acc = -_softplus(-(acc + b_ref[...]))
    for o_ref in o_refs:
        o_ref[...] = acc.astype(o_ref.dtype)


def norm_matmul(x, norm_w, w, *, tm, tn, cols=None, out_dtypes=(F32,), logsig_bias=None, emit_bf16=False,
                out_scale=None):
    m, d = x.shape
    col0, n = cols if cols is not None else (0, w.shape[1])
    tn = min(tn, n)
    first = col0 // tn
    assert first * tn == col0 and n % tn == 0 and m % tm == 0 and (m == tm or not emit_bf16)
    has_bias = logsig_bias is not None
    has_norm = norm_w is not None
    if not has_norm:
        assert x.dtype == BF16
        norm_w = jnp.ones((d,), F32)
    in_specs = [
        pl.BlockSpec((tm, d), lambda i, j: (i, 0)),
        pl.BlockSpec((1, d), lambda i, j: (0, 0)),
        pl.BlockSpec((d, tn), lambda i, j: (0, first + j)),
    ]
    args = [x, norm_w.reshape(1, d), w]
    if has_bias:
        in_specs.append(pl.BlockSpec((1, tn), lambda i, j: (0, j)))
        args.append(logsig_bias.reshape(1, n))
    out_specs = [pl.BlockSpec((tm, tn), lambda i, j: (i, j)) for _ in out_dtypes]
    out_shape = [jax.ShapeDtypeStruct((m, n), dt) for dt in out_dtypes]
    if emit_bf16:
        out_specs.append(pl.BlockSpec((d, tn), lambda i, j: (0, j)))
        out_shape.append(jax.ShapeDtypeStruct((d, n), BF16))
    outs = pl.pallas_call(
        functools.partial(_norm_matmul_kernel, has_norm=has_norm, has_bias=has_bias, n_out=len(out_dtypes),
                          emit_bf16=emit_bf16, out_scale=out_scale),
        grid=(m // tm, n // tn),
        in_specs=in_specs,
        out_specs=out_specs,
        out_shape=out_shape,
        scratch_shapes=[pltpu.VMEM((tm, d), BF16)] if has_norm else [],
        compiler_params=_cparams("parallel", "arbitrary"),
        name="norm_matmul",
    )(*args)
    return outs[0] if len(outs) == 1 else outs


def _matmul_norm_res_kernel(a_ref, w_ref, post_ref, res_ref, o_ref):
    k = pl.program_id(1)

    @pl.when(k == 0)
    def _():
        o_ref[...] = jnp.zeros_like(o_ref)

    o_ref[...] += _dot(a_ref[...], w_ref[...])

    @pl.when(k == pl.num_programs(1) - 1)
    def _():
        o_ref[...] = res_ref[...] + _rms(o_ref[...], post_ref[...])


def matmul_norm_residual(a, w, post_w, res, *, tm, tk):
    m, kdim = a.shape
    n = w.shape[1]
    return pl.pallas_call(
        _matmul_norm_res_kernel,
        grid=(m // tm, kdim // tk),
        in_specs=[
            pl.BlockSpec((tm, tk), lambda i, k: (i, k)),
            pl.BlockSpec((tk, n), lambda i, k: (k, 0)),
            pl.BlockSpec((1, n), lambda i, k: (0, 0)),
            pl.BlockSpec((tm, n), lambda i, k: (i, 0)),
        ],
        out_specs=pl.BlockSpec((tm, n), lambda i, k: (i, 0)),
        out_shape=jax.ShapeDtypeStruct((m, n), F32),
        compiler_params=_cparams("parallel", "arbitrary"),
        name="matmul_norm_residual",
    )(a, w, post_w.reshape(1, n), res)


SUBLANES = 8
MXU_COLS = 256


def _proj_conv_kernel(x_ref, w_ref, cw_ref, cb_ref, act_ref, tail_ref, carry_ref, *, blocks_per_seq):
    i, j = pl.program_id(0), pl.program_id(1)
    tm, tn = act_ref.shape

    @pl.when(i % blocks_per_seq == 0)
    def _():
        carry_ref[j] = jnp.zeros((SUBLANES, tn), F32)

    x = x_ref[...]
    head_row = _iota((SUBLANES, MXU_COLS), 0)
    for c0 in range(0, tn, MXU_COLS):
        cols = slice(c0, c0 + MXU_COLS)
        raw = _dot(x, w_ref[:, cols])
        prev = carry_ref[j, :, cols]
        acc = cb_ref[:, cols]
        for tap in range(CONV_W):
            back = CONV_W - 1 - tap
            if back == 0:
                xs = raw
            else:
                rolled = pltpu.roll(raw, back, 0)
                head = jnp.where(head_row < back, pltpu.roll(prev, back, 0), rolled[:SUBLANES])
                xs = jnp.concatenate([head, rolled[SUBLANES:]], axis=0)
            acc = acc + xs * cw_ref[tap:tap + 1, cols]
        act_ref[:, cols] = _silu(acc)
        tail_ref[0, :, cols] = raw[tm - (CONV_W - 1):, :]
        carry_ref[j, :, cols] = raw[tm - SUBLANES:, :]


def proj_conv(xn, w, conv_w, conv_b, *, batch, tm, tn):
    t, d = xn.shape
    c = w.shape[1]
    blocks_per_seq = t // batch // tm
    assert blocks_per_seq * tm * batch == t and c % tn == 0
    act, tails = pl.pallas_call(
        functools.partial(_proj_conv_kernel, blocks_per_seq=blocks_per_seq),
        grid=(t // tm, c // tn),
        in_specs=[
            pl.BlockSpec((tm, d), lambda i, j: (i, 0)),
            pl.BlockSpec((d, tn), lambda i, j: (0, j)),
            pl.BlockSpec((CONV_W, tn), lambda i, j: (0, j)),
            pl.BlockSpec((1, tn), lambda i, j: (0, j)),
        ],
        out_specs=[
            pl.BlockSpec((tm, tn), lambda i, j: (i, j)),
            pl.BlockSpec((1, CONV_W - 1, tn), lambda i, j: (i, 0, j)),
        ],
        out_shape=[
            jax.ShapeDtypeStruct((t, c), F32),
            jax.ShapeDtypeStruct((t // tm, CONV_W - 1, c), F32),
        ],
        scratch_shapes=[pltpu.VMEM((c // tn, SUBLANES, tn), F32)],
        compiler_params=_cparams("arbitrary", "arbitrary"),
        name="proj_conv",
    )(xn, w, conv_w, conv_b.reshape(1, c))
    return act, tails[blocks_per_seq - 1::blocks_per_seq]


def _ssd_prep_kernel(dtr_ref, bias_ref, alog_ref, dt_ref, acs_ref):
    dt = _softplus(dtr_ref[...] + bias_ref[...])
    da = dt * (-jnp.exp(alog_ref[...]))
    lower = (_iota((CHUNK, CHUNK), 1) <= _iota((CHUNK, CHUNK), 0)).astype(F32)
    dt_ref[...] = dt
    acs_ref[...] = _dot_exact(lower, da)


def ssd_prep(dt_raw, dt_bias, a_log):
    t, nh = dt_raw.shape
    blk = pl.BlockSpec((CHUNK, nh), lambda c: (c, 0))
    vec = pl.BlockSpec((1, nh), lambda c: (0, 0))
    return pl.pallas_call(
        _ssd_prep_kernel,
        grid=(t // CHUNK,),
        in_specs=[blk, vec, vec],
        out_specs=[blk, blk],
        out_shape=[jax.ShapeDtypeStruct((t, nh), F32)] * 2,
        compiler_params=_cparams("parallel"),
        name="ssd_prep",
    )(dt_raw, dt_bias.reshape(1, nh), a_log.reshape(1, nh))


def _ssd_kernel(x_ref, b_ref, c_ref, z_ref, acsc_ref, acst_ref, dtt_ref, d_ref, gw_ref,
                y_ref, hfin_ref, ht_ref, yd_ref, *, heads_per_group):
    c = pl.program_id(2)
    p_n = SSM_HEAD_DIM
    pair = LANES // p_n
    assert CHUNK == D_STATE == LANES and heads_per_group % pair == 0

    @pl.when(c == 0)
    def _():
        ht_ref[...] = jnp.zeros_like(ht_ref)

    bm = b_ref[...]
    cm = c_ref[...]
    bm_t = bm.T
    cb = lax.dot_general(cm.astype(BF16), bm.astype(BF16), NT_DIMS, preferred_element_type=F32)
    causal = _iota((CHUNK, CHUNK), 1) <= _iota((CHUNK, CHUNK), 0)
    lane_head = _iota((CHUNK, LANES), 1) // p_n
    acsc = acsc_ref[0, 0]
    acst = acst_ref[0, 0]
    dtt = dtt_ref[0, 0]

    for slab in range(heads_per_group // pair):
        lanes = slice(slab * LANES, (slab + 1) * LANES)
        x_bf = x_ref[:, lanes].astype(BF16)
        ht = ht_ref[:, lanes]
        rhs = jnp.concatenate([x_bf, ht.astype(BF16)], axis=0)
        y_slab = s_slab = keep = None
        for k in range(pair):
            r = slab * pair + k
            a_col = jnp.broadcast_to(acsc[:, r:r + 1], (CHUNK, CHUNK))
            a_row = acst[r:r + 1, :]
            dt_row = dtt[r:r + 1, :]
            decay = jnp.exp(jnp.where(causal, a_col - a_row, -jnp.inf))
            within = cb * decay * dt_row
            carried = cm * jnp.exp(a_col)
            lhs = jnp.concatenate([within, carried], axis=1).astype(BF16)
            y_r = _dot(lhs, rhs)
            a_last = a_row[:, CHUNK - 1:CHUNK]
            to_end = jnp.exp(a_last - a_row) * dt_row
            s_r = _dot((bm_t * to_end).astype(BF16), x_bf)
            k_r = jnp.broadcast_to(jnp.exp(a_last), (D_STATE, LANES))
            if k == 0:
                y_slab, s_slab, keep = y_r, s_r, k_r
            else:
                mine = lane_head == k
                y_slab = jnp.where(mine, y_r, y_slab)
                s_slab = jnp.where(mine, s_r, s_slab)
                keep = jnp.where(mine, k_r, keep)
        yd_ref[:, lanes] = y_slab
        ht_ref[:, lanes] = ht * keep + s_slab

    y = yd_ref[...] + d_ref[...] * x_ref[...]
    y = y * _silu(z_ref[...])
    y_ref[...] = _rms(y, gw_ref[...]).astype(y_ref.dtype)

    @pl.when(c == pl.num_programs(2) - 1)
    def _():
        hfin_ref[0] = ht_ref[...].T


def ssd_prompt(xbc_act, z, dt, acs, d_exp, gnorm_w, *, batch):
    t, conv_dim = xbc_act.shape
    d_inner = z.shape[1]
    nh = dt.shape[1]
    groups = SSM_GROUPS
    r_n = nh // groups
    gp = d_inner // groups
    nc = t // batch // CHUNK
    n_chunks = t // CHUNK
    per_group = lambda a: a.reshape(n_chunks, CHUNK, groups, r_n).transpose(0, 2, 1, 3)
    acs_g = per_group(acs)
    acs_gt = acs_g.transpose(0, 1, 3, 2)
    dt_gt = per_group(dt).transpose(0, 1, 3, 2)
    xoff = d_inner // D_STATE
    row = lambda b, g, c: b * nc + c
    col_form = pl.BlockSpec((1, 1, CHUNK, r_n), lambda b, g, c: (row(b, g, c), g, 0, 0))
    row_form = pl.BlockSpec((1, 1, r_n, CHUNK), lambda b, g, c: (row(b, g, c), g, 0, 0))
    y, hfin = pl.pallas_call(
        functools.partial(_ssd_kernel, heads_per_group=r_n),
        grid=(batch, groups, nc),
        in_specs=[
            pl.BlockSpec((CHUNK, gp), lambda b, g, c: (row(b, g, c), g)),
            pl.BlockSpec((CHUNK, D_STATE), lambda b, g, c: (row(b, g, c), xoff + g)),
            pl.BlockSpec((CHUNK, D_STATE), lambda b, g, c: (row(b, g, c), xoff + groups + g)),
            pl.BlockSpec((CHUNK, gp), lambda b, g, c: (row(b, g, c), g)),
            col_form, row_form, row_form,
            pl.BlockSpec((1, gp), lambda b, g, c: (0, g)),
            pl.BlockSpec((1, gp), lambda b, g, c: (0, g)),
        ],
        out_specs=[
            pl.BlockSpec((CHUNK, gp), lambda b, g, c: (row(b, g, c), g)),
            pl.BlockSpec((1, gp, D_STATE), lambda b, g, c: (b, g, 0)),
        ],
        out_shape=[
            jax.ShapeDtypeStruct((t, d_inner), BF16),
            jax.ShapeDtypeStruct((batch, d_inner, D_STATE), F32),
        ],
        scratch_shapes=[pltpu.VMEM((D_STATE, gp), F32), pltpu.VMEM((CHUNK, gp), F32)],
        compiler_params=_cparams("parallel", "parallel", "arbitrary"),
        name="ssd_prompt",
    )(xbc_act, xbc_act, xbc_act, z, acs_g, acs_gt, dt_gt, d_exp.reshape(1, d_inner),
      gnorm_w.reshape(1, d_inner))
    return y, hfin


def _conv_step_kernel(st_ref, new_ref, w_ref, b_ref, act_ref, nst_ref):
    xn = new_ref[...]
    acc = b_ref[...]
    for tap in range(CONV_W - 1):
        acc = acc + st_ref[tap] * w_ref[tap:tap + 1, :]
        if tap > 0:
            nst_ref[tap - 1] = st_ref[tap]
    acc = acc + xn * w_ref[CONV_W - 1:CONV_W, :]
    nst_ref[CONV_W - 2] = xn
    act_ref[...] = _silu(acc)


def conv_step(state_t, xbc_new, conv_w, conv_b, *, tc):
    taps, batch, c = state_t.shape
    return pl.pallas_call(
        _conv_step_kernel,
        grid=(c // tc,),
        in_specs=[
            pl.BlockSpec((taps, batch, tc), lambda j: (0, 0, j)),
            pl.BlockSpec((batch, tc), lambda j: (0, j)),
            pl.BlockSpec((CONV_W, tc), lambda j: (0, j)),
            pl.BlockSpec((1, tc), lambda j: (0, j)),
        ],
        out_specs=[
            pl.BlockSpec((batch, tc), lambda j: (0, j)),
            pl.BlockSpec((taps, batch, tc), lambda j: (0, 0, j)),
        ],
        out_shape=[
            jax.ShapeDtypeStruct((batch, c), F32),
            jax.ShapeDtypeStruct((taps, batch, c), F32),
        ],
        compiler_params=_cparams("parallel"),
        name="conv_step",
    )(state_t, xbc_new, conv_w, conv_b.reshape(1, c))


def _ssd_step_kernel(h_ref, xs_ref, b_ref, c_ref, z_ref, dtr_ref, bias_ref, alog_ref, d_ref, gw_ref,
                     ho_ref, y_ref, dte_ref, dece_ref, *, heads_per_group):
    g = pl.program_id(0)
    r_n, p_n = heads_per_group, SSM_HEAD_DIM
    gp = r_n * p_n
    batch, nh = dtr_ref.shape
    rows_per_tile = LANES // p_n

    expand = (_iota((nh, gp), 1) // p_n + g * r_n == _iota((nh, gp), 0)).astype(F32)
    dt = _softplus(dtr_ref[...] + bias_ref[...])
    dte_ref[...] = _dot_exact(dt, expand)
    dece_ref[...] = _dot_exact(jnp.exp(dt * (-jnp.exp(alog_ref[...]))), expand)

    def body(b, carry):
        x = xs_ref[pl.ds(b, 1), :]
        bv = b_ref[pl.ds(b, 1), :]
        cv = c_ref[pl.ds(b, 1), :]
        xdt = x * dte_ref[pl.ds(b, 1), :]
        dec = dece_ref[pl.ds(b, 1), :]
        y_parts = []
        for j in range(gp // LANES):
            lanes = slice(j * LANES, (j + 1) * LANES)
            heads = pl.ds(j * rows_per_tile, rows_per_tile)
            h = h_ref[b, heads].reshape(LANES, D_STATE)
            h_new = h * _col_bcast(dec[:, lanes]) + _col_bcast(xdt[:, lanes]) * bv
            ho_ref[b, heads] = h_new.reshape(rows_per_tile, p_n, D_STATE)
            ycol = jnp.sum(h_new * cv, axis=-1, keepdims=True)
            y_parts.append(jnp.broadcast_to(ycol, (LANES, LANES)).T[0:1, :])
        y = jnp.concatenate(y_parts, axis=1) + d_ref[...] * x
        y = y * _silu(z_ref[pl.ds(b, 1), :])
        y_ref[pl.ds(b, 1), :] = _rms(y, gw_ref[...])
        return carry

    lax.fori_loop(0, batch, body, 0)


def ssd_step(h, xbc_act, z, dt_raw, dt_bias, a_log, d_exp, gnorm_w):
    batch, nh, p_n, n_n = h.shape
    d_inner = nh * p_n
    groups = SSM_GROUPS
    r_n = nh // groups
    gp = d_inner // groups
    xoff = d_inner // D_STATE
    return pl.pallas_call(
        functools.partial(_ssd_step_kernel, heads_per_group=r_n),
        grid=(groups,),
        in_specs=[
            pl.BlockSpec((batch, r_n, p_n, n_n), lambda g: (0, g, 0, 0)),
            pl.BlockSpec((batch, gp), lambda g: (0, g)),
            pl.BlockSpec((batch, D_STATE), lambda g: (0, xoff + g)),
            pl.BlockSpec((batch, D_STATE), lambda g: (0, xoff + groups + g)),
            pl.BlockSpec((batch, gp), lambda g: (0, g)),
            pl.BlockSpec((batch, nh), lambda g: (0, 0)),
            pl.BlockSpec((1, nh), lambda g: (0, 0)),
            pl.BlockSpec((1, nh), lambda g: (0, 0)),
            pl.BlockSpec((1, gp), lambda g: (0, g)),
            pl.BlockSpec((1, gp), lambda g: (0, g)),
        ],
        out_specs=[
            pl.BlockSpec((batch, r_n, p_n, n_n), lambda g: (0, g, 0, 0)),
            pl.BlockSpec((batch, gp), lambda g: (0, g)),
        ],
        out_shape=[
            jax.ShapeDtypeStruct(h.shape, F32),
            jax.ShapeDtypeStruct((batch, d_inner), F32),
        ],
        scratch_shapes=[pltpu.VMEM((batch, gp), F32), pltpu.VMEM((batch, gp), F32)],
        compiler_params=_cparams("parallel"),
        name="ssd_step",
    )(h, xbc_act, xbc_act, xbc_act, z, dt_raw, dt_bias.reshape(1, nh), a_log.reshape(1, nh),
      d_exp.reshape(1, d_inner), gnorm_w.reshape(1, d_inner))


def _cumsum_kernel(x_ref, o_ref, *, tile):
    seq = x_ref.shape[1]
    lower = (_iota((tile, tile), 1) <= _iota((tile, tile), 0)).astype(F32)
    carry = jnp.zeros((1, x_ref.shape[2]), F32)
    for i in range(seq // tile):
        cs = _dot_exact(lower, x_ref[0, i * tile:(i + 1) * tile, :]) + carry
        o_ref[0, i * tile:(i + 1) * tile, :] = cs
        carry = cs[tile - 1:tile, :]


def cumsum_seq(x3, *, tile=256):
    batch, seq, n = x3.shape
    blk = pl.BlockSpec((1, seq, n), lambda b: (b, 0, 0))
    return pl.pallas_call(
        functools.partial(_cumsum_kernel, tile=tile),
        grid=(batch,),
        in_specs=[blk],
        out_specs=blk,
        out_shape=jax.ShapeDtypeStruct(x3.shape, F32),
        compiler_params=_cparams("parallel"),
        name="cumsum_seq",
    )(x3)


def _fox_prompt_kernel(q_ref, k_ref, v_ref, c_ref, o_ref, *, tq):
    q = q_ref[...]
    sub = tq // LANES
    n_q = k_ref.shape[0] // tq

    def block(ki, carry, cq, masked):
        m, l, acc = carry
        k = k_ref[ki * tq:(ki + 1) * tq, :]
        v = v_ref[ki * tq:(ki + 1) * tq, :]
        ck_rows = c_ref[0, ki * sub:(ki + 1) * sub, :] * LOG2E
        ck = jnp.concatenate([jnp.broadcast_to(ck_rows[j:j + 1, :], (tq, LANES)) for j in range(sub)], axis=1)
        s = lax.dot_general(q, k, NT_DIMS, preferred_element_type=F32)
        s = s + cq - ck
        if masked:
            s = jnp.where(_iota((tq, tq), 1) <= _iota((tq, tq), 0), s, -jnp.inf)
        m_new = jnp.maximum(m, jnp.max(s, axis=-1, keepdims=True))
        alpha = jnp.exp2(m - m_new)
        p = jnp.exp2(s - m_new)
        l = alpha * l + jnp.sum(p, axis=-1, keepdims=True)
        acc = alpha * acc + _dot(p.astype(BF16), v)
        return m_new, l, acc

    for n_before in range(n_q):
        @pl.when(pl.program_id(2) == n_before)
        def _(n_before=n_before):
            cq_rows = c_ref[0, n_before * sub:(n_before + 1) * sub, :] * LOG2E
            cq = jnp.concatenate([_col_bcast(cq_rows[i:i + 1, :]) for i in range(sub)], axis=0)
            cq = jnp.concatenate([cq] * sub, axis=1)
            carry = (jnp.full((tq, 1), -jnp.inf, F32), jnp.zeros((tq, 1), F32), jnp.zeros((tq, q.shape[1]), F32))
            for ki in range(n_before):
                carry = block(ki, carry, cq, False)
            _, l, acc = block(n_before, carry, cq, True)
            o_ref[...] = (acc / l).astype(o_ref.dtype)


def fox_prompt_q_scale(dh):
    return LOG2E / (dh ** 0.5)


def fox_prompt(q, k, v, c_tab, *, batch, heads, tq):
    t, hd = q.shape
    dh = hd // heads
    seq = t // batch
    nq = seq // tq
    return pl.pallas_call(
        functools.partial(_fox_prompt_kernel, tq=tq),
        grid=(batch, heads, nq),
        in_specs=[
            pl.BlockSpec((tq, dh), lambda b, h, i: (b * nq + i, h)),
            pl.BlockSpec((seq, dh), lambda b, h, i: (b, h)),
            pl.BlockSpec((seq, dh), lambda b, h, i: (b, h)),
            pl.BlockSpec((1, seq // LANES, LANES), lambda b, h, i: (b * heads + h, 0, 0)),
        ],
        out_specs=pl.BlockSpec((tq, dh), lambda b, h, i: (b * nq + i, h)),
        out_shape=jax.ShapeDtypeStruct((t, hd), BF16),
        compiler_params=_cparams("parallel", "parallel", "arbitrary"),
        name="fox_prompt",
    )(q, k, v, c_tab)


BIAS_PAGES_PER_STEP = 16
ATTN_PAGES_PER_STEP = 4


def _visit_page(pt, batch_row, visit, n_pages):
    return pt[batch_row * n_pages + (n_pages - 1 - visit)]


def _forget_bias_kernel(*refs, n_sub):
    lfn_ref = refs[1]
    lf_refs = refs[2:2 + n_sub]
    o_ref, run_ref = refs[2 + n_sub:]
    page = lf_refs[0].shape[1]

    @pl.when(pl.program_id(1) == 0)
    def _():
        run_ref[...] = lfn_ref[0]

    later = (_iota((page, page), 1) > _iota((page, page), 0)).astype(F32)
    run = run_ref[...]
    for j in range(n_sub):
        lf = lf_refs[j][0]
        o_ref[0, j] = _dot_exact(later, lf) + run
        run = run + jnp.sum(lf, axis=0, keepdims=True)
    run_ref[...] = run


def forget_bias(logf_new, cache_logf, page_table):
    batch, heads = logf_new.shape
    n_phys, page, _ = cache_logf.shape
    n_pages = page_table.shape[1]
    n_sub = BIAS_PAGES_PER_STEP

    def page_spec(j):
        return pl.BlockSpec((1, page, heads), lambda b, s, pt: (_visit_page(pt, b, s * n_sub + j, n_pages), 0, 0))

    grid_spec = pltpu.PrefetchScalarGridSpec(
        num_scalar_prefetch=1,
        grid=(batch, n_pages // n_sub),
        in_specs=[pl.BlockSpec((1, 1, heads), lambda b, s, pt: (b, 0, 0))] + [page_spec(j) for j in range(n_sub)],
        out_specs=pl.BlockSpec((1, n_sub, page, heads), lambda b, s, pt: (b, s, 0, 0)),
        scratch_shapes=[pltpu.VMEM((1, heads), F32)],
    )
    return pl.pallas_call(
        functools.partial(_forget_bias_kernel, n_sub=n_sub),
        grid_spec=grid_spec,
        out_shape=jax.ShapeDtypeStruct((batch, n_pages, page, heads), F32),
        compiler_params=_cparams("parallel", "arbitrary"),
        name="forget_bias",
    )(page_table.reshape(-1), logf_new.reshape(batch, 1, heads), *([cache_logf] * n_sub))


def _fox_sample_kernel(*refs, n_sub, scale):
    q_ref, kn_ref, vn_ref = refs[1:4]
    page_refs = refs[4:4 + 3 * n_sub]
    o_ref, m_ref, l_ref, acc_ref = refs[4 + 3 * n_sub:]
    step = pl.program_id(1)
    heads, dh = q_ref.shape[1:]
    rows = page_refs[0].shape[1] * heads
    own_head = _iota((heads, rows), 1) % heads == _iota((heads, rows), 0)

    @pl.when(step == 0)
    def _():
        m_ref[...] = jnp.full_like(m_ref, -jnp.inf)
        l_ref[...] = jnp.zeros_like(l_ref)
        acc_ref[...] = jnp.zeros_like(acc_ref)

    q = q_ref[0].astype(BF16)
    m, l, acc = m_ref[...], l_ref[...], acc_ref[...]
    for j in range(n_sub):
        ck_ref, cv_ref, bias_ref = page_refs[3 * j:3 * j + 3]
        k = ck_ref[0].reshape(rows, dh).astype(BF16)
        v = cv_ref[0].reshape(rows, dh).astype(BF16)
        s = lax.dot_general(q, k, NT_DIMS, preferred_element_type=F32) * scale + bias_ref[0, 0]
        s = jnp.where(own_head, s, -jnp.inf)
        m_new = jnp.maximum(m, jnp.max(s, axis=-1, keepdims=True))
        alpha = jnp.exp(m - m_new)
        pr = jnp.exp(s - m_new)
        l = alpha * l + jnp.sum(pr, axis=-1, keepdims=True)
        acc = alpha * acc + _dot(pr.astype(BF16), v)
        m = m_new
    m_ref[...], l_ref[...], acc_ref[...] = m, l, acc

    @pl.when(step == pl.num_programs(1) - 1)
    def _():
        qf = q.astype(F32)
        kn = kn_ref[0].astype(BF16).astype(F32)
        vn = vn_ref[0].astype(BF16).astype(F32)
        s_new = jnp.sum(qf * kn, axis=-1, keepdims=True) * scale
        m_fin = jnp.maximum(m, s_new)
        a_fin = jnp.exp(m - m_fin)
        p_new = jnp.exp(s_new - m_fin)
        l_fin = a_fin * l + p_new
        o_ref[0] = (a_fin * acc + p_new.astype(BF16).astype(F32) * vn) / l_fin


def fox_sample(q, k_new, v_new, logf_new, cache_k, cache_v, cache_logf, page_table):
    batch, heads, dh = q.shape
    page = cache_k.shape[1]
    n_pages = page_table.shape[1]
    n_sub = ATTN_PAGES_PER_STEP
    bias = forget_bias(logf_new, cache_logf, page_table)
    bias = bias.reshape(batch, n_pages, 1, page * heads)

    new_spec = pl.BlockSpec((1, heads, dh), lambda b, s, pt: (b, 0, 0))
    in_specs = [new_spec, new_spec, new_spec]
    operands = [q, k_new, v_new]
    for j in range(n_sub):
        phys = lambda b, s, pt, j=j: (_visit_page(pt, b, s * n_sub + j, n_pages), 0, 0, 0)
        in_specs += [
            pl.BlockSpec((1, page, heads, dh), phys),
            pl.BlockSpec((1, page, heads, dh), phys),
            pl.BlockSpec((1, 1, 1, page * heads), lambda b, s, pt, j=j: (b, s * n_sub + j, 0, 0)),
        ]
        operands += [cache_k, cache_v, bias]
    grid_spec = pltpu.PrefetchScalarGridSpec(
        num_scalar_prefetch=1,
        grid=(batch, n_pages // n_sub),
        in_specs=in_specs,
        out_specs=pl.BlockSpec((1, heads, dh), lambda b, s, pt: (b, 0, 0)),
        scratch_shapes=[pltpu.VMEM((heads, 1), F32), pltpu.VMEM((heads, 1), F32), pltpu.VMEM((heads, dh), F32)],
    )
    return pl.pallas_call(
        functools.partial(_fox_sample_kernel, n_sub=n_sub, scale=1.0 / (dh ** 0.5)),
        grid_spec=grid_spec,
        out_shape=jax.ShapeDtypeStruct((batch, heads, dh), F32),
        compiler_params=_cparams("parallel", "arbitrary"),
        name="fox_sample",
    )(page_table.reshape(-1), *operands)


def _tiles(m):
    return min(m, 512)


def _trunk(y, w, *, mamba_mixer, fox_mixer, q_scale=None, xbc_proj=None):
    m, d = y.shape
    tm = _tiles(m)
    nw = w["norm_w"]
    extras = {}

    def ffn(y, layer, half, next_norm_w=None):
        pre, post = nw[layer, 4 * half], nw[layer, 4 * half + 1]
        kw = dict(tm=tm, tf=FFN_TF, next_norm_w=next_norm_w)
        if (layer, half) in w["ffn_bf16"]:
            w_in, w_out = w["ffn_bf16"][layer, half]
            return ffn_half(y, pre, post, w_in, w_out, None, **kw)
        y, w["ffn_bf16"][layer, half], *rest = ffn_half(y, pre, post, w["ffn_in"], w["ffn_out"], (layer, half), **kw)
        return (y, *rest) if rest else y

    def proj(xn, name, cols, out_dtypes=(F32,), out_scale=None):
        key = (name, cols)
        kw = dict(tm=tm, tn=PROJ_TN, out_dtypes=out_dtypes, out_scale=out_scale)
        if key in w["proj_bf16"]:
            return norm_matmul(xn, None, w["proj_bf16"][key], **kw)
        *outs, w["proj_bf16"][key] = norm_matmul(xn, None, w[name], cols=cols, emit_bf16=True, **kw)
        return outs[0] if len(outs) == 1 else outs

    hd = w["w_q"].shape[1]
    d_inner, conv_dim, nh = w["in_split"]
    y, xn = ffn(y, 0, 0, next_norm_w=nw[0, 2])
    z = proj(xn, "in_proj", (0, d_inner))
    xbc = proj(xn, "in_proj", (d_inner, conv_dim)) if xbc_proj is None else xbc_proj(xn)
    dt_raw = proj(xn, "in_proj", (d_inner + conv_dim, nh))
    mixed, extras["conv"], extras["ssm"] = mamba_mixer(z, xbc, dt_raw)
    y = matmul_norm_residual(mixed, w["out_proj"], nw[0, 3], y, tm=tm, tk=PROJ_TK)
    y, xn = ffn(y, 0, 1, next_norm_w=w["kv_norm_w"])
    extras["k"], extras["k_bf"] = proj(xn, "w_kv", (0, hd), out_dtypes=(F32, BF16))
    extras["v"], extras["v_bf"] = proj(xn, "w_kv", (hd, hd), out_dtypes=(F32, BF16))
    extras["logf"] = norm_matmul(xn, None, w["w_f"], tm=tm, tn=LANES, logsig_bias=w["b_f"])
    y, xn = ffn(y, 1, 0, next_norm_w=nw[1, 2])
    q = proj(xn, "w_q", (0, hd), out_dtypes=(BF16,), out_scale=q_scale)
    attn = fox_mixer(q, extras)
    y = matmul_norm_residual(attn, w["w_o"], nw[1, 3], y, tm=tm, tk=PROJ_TK)
    y = ffn(y, 1, 1)
    return y, extras


def kernel(x_prompt, x_sample, state_conv, state_ssm, cache_k, cache_v, cache_logf, page_table,
           norm_w, ffn_w_in, ffn_w_out, m_in_proj, m_conv_w, m_conv_b, m_dt_bias, m_A_log, m_D,
           m_gnorm_w, m_out_proj, kv_norm_w, w_kvf, b_fg, w_q, w_o):
    bp, seq, d = x_prompt.shape
    bs = x_sample.shape[0]
    heads = cache_k.shape[2]
    dh = cache_k.shape[3]
    hd = heads * dh
    nh = m_dt_bias.shape[1]
    d_inner = nh * SSM_HEAD_DIM
    conv_dim = m_conv_w.shape[2]

    pad_f = LANES - heads
    w = {
        "norm_w": norm_w,
        "ffn_in": ffn_w_in,
        "ffn_out": ffn_w_out,
        "ffn_bf16": {},
        "in_proj": m_in_proj[0],
        "in_split": (d_inner, conv_dim, nh),
        "w_kv": w_kvf,
        "w_q": w_q[0],
        "proj_bf16": {},
        "out_proj": m_out_proj[0].astype(BF16),
        "w_o": w_o[0].astype(BF16),
        "kv_norm_w": kv_norm_w,
        "w_f": jnp.pad(w_kvf[:, 2 * hd:], ((0, 0), (0, pad_f))).astype(BF16),
        "b_f": jnp.pad(b_fg, (0, pad_f)),
    }
    d_exp = jnp.repeat(m_D[0], SSM_HEAD_DIM)
    conv_w, conv_b = m_conv_w[0], m_conv_b[0]

    def xbc_conv_prompt(xn):
        w_xbc = w["proj_bf16"]["in_proj", (d_inner, conv_dim)]
        return proj_conv(xn, w_xbc, conv_w, conv_b, batch=bp, tm=_tiles(bp * seq), tn=PROJ_TN)

    def mamba_prompt(z, xbc_act_tail, dt_raw):
        act, tail = xbc_act_tail
        dt, acs = ssd_prep(dt_raw, m_dt_bias[0], m_A_log[0])
        yb, hfin = ssd_prompt(act, z, dt, acs, d_exp, m_gnorm_w[0], batch=bp)
        return yb, tail, hfin.reshape(bp, nh, SSM_HEAD_DIM, D_STATE)

    def fox_prompt_mixer(q, ex):
        logf3 = ex["logf"].reshape(bp, seq, LANES)
        c = cumsum_seq(logf3)[:, :, :heads]
        c_tab = c.transpose(0, 2, 1).reshape(bp * heads, seq // LANES, LANES)
        return fox_prompt(q, ex["k_bf"], ex["v_bf"], c_tab, batch=bp, heads=heads, tq=512)

    def mamba_sample(z, xbc, dt_raw):
        st = state_conv[0].transpose(1, 0, 2)
        act, nst = conv_step(st, xbc, conv_w, conv_b, tc=2048)
        h_new, yg = ssd_step(state_ssm[0], act, z, dt_raw, m_dt_bias[0], m_A_log[0], d_exp, m_gnorm_w[0])
        yg = jnp.pad(yg, ((0, SAMPLE_ROWS - bs), (0, 0))).astype(BF16)
        return yg, nst.transpose(1, 0, 2), h_new

    def fox_sample_mixer(q, ex):
        per_head = lambda a: a[:bs].astype(F32).reshape(bs, heads, dh)
        attn = fox_sample(per_head(q), per_head(ex["k"]), per_head(ex["v"]), ex["logf"][:bs, :heads],
                          cache_k, cache_v, cache_logf, page_table)
        return jnp.pad(attn.reshape(bs, hd), ((0, SAMPLE_ROWS - bs), (0, 0))).astype(BF16)

    xs_rows = jnp.pad(x_sample.reshape(bs, d), ((0, SAMPLE_ROWS - bs), (0, 0)))
    y_s, ex_s = _trunk(xs_rows, w, mamba_mixer=mamba_sample, fox_mixer=fox_sample_mixer)
    y_p, ex_p = _trunk(x_prompt.reshape(bp * seq, d), w, mamba_mixer=mamba_prompt, fox_mixer=fox_prompt_mixer,
                       q_scale=fox_prompt_q_scale(dh), xbc_proj=xbc_conv_prompt)

    return (
        y_p.reshape(bp, seq, d),
        y_s[:bs].reshape(bs, 1, d),
        ex_p["conv"][None],
        ex_p["ssm"][None],
        ex_p["k"].reshape(bp, seq, heads, dh),
        ex_p["v"].reshape(bp, seq, heads, dh),
        ex_p["logf"][:, :heads].reshape(bp, seq, heads),
        ex_s["conv"][None],
        ex_s["ssm"][None],
        ex_s["k"][:bs].reshape(bs, 1, heads, dh),
        ex_s["v"][:bs].reshape(bs, 1, heads, dh),
        ex_s["logf"][:bs, :heads].reshape(bs, 1, heads),
    )
```

```python
import functools

import jax
import jax.numpy as jnp
from jax import lax
from jax.experimental import pallas as pl
from jax.experimental.pallas import tpu as pltpu

F32 = jnp.float32
BF16 = jnp.bfloat16
HIGHEST = lax.Precision.HIGHEST
EPS = 1e-6
LOG2E = 1.4426950408889634

SSM_HEAD_DIM = 64
SSM_GROUPS = 8
D_STATE = 128
CONV_W = 4
CHUNK = 128

LANES = 128
SAMPLE_ROWS = 16
ROW_TILE = 512
PROJ_TN = 1024
PROJ_TK = 512
FFN_TF = 256
FOX_TQ = 512
CONV_STEP_TC = 2048
VMEM_LIMIT = 56 * 1024 * 1024

NT_DIMS = (((1,), (1,)), ((), ()))


def _cparams(*semantics):
    return pltpu.CompilerParams(dimension_semantics=semantics, vmem_limit_bytes=VMEM_LIMIT)


def _rms(x, w):
    return x * lax.rsqrt(jnp.mean(x * x, axis=-1, keepdims=True) + EPS) * w


def _silu(x):
    return x * jax.nn.sigmoid(x)


def _softplus(x):
    return jnp.maximum(x, 0.0) + jnp.log1p(jnp.exp(-jnp.abs(x)))


def _dot(a, b):
    return jnp.dot(a, b, preferred_element_type=F32)


def _dot_exact(a, b):
    return jnp.dot(a, b, precision=HIGHEST, preferred_element_type=F32)


def _iota(shape, axis):
    return lax.broadcasted_iota(jnp.int32, shape, axis)


def _col_bcast(row):
    return jnp.broadcast_to(row, (LANES, LANES)).T


def _ffn_kernel(*refs, emit_bf16, has_next):
    x_ref, pre_ref, post_ref, wg_ref, wu_ref, wo_ref = refs[:6]
    n_in = 6 + has_next
    o_ref = refs[n_in]
    w_out_refs = refs[n_in + 1:n_in + 4] if emit_bf16 else ()
    xn_ref = refs[-1]
    f = pl.program_id(1)

    @pl.when(f == 0)
    def _():
        xn_ref[...] = _rms(x_ref[...], pre_ref[...]).astype(BF16)
        o_ref[...] = jnp.zeros_like(o_ref)

    wg, wu, wo = (r[...].astype(BF16) for r in (wg_ref, wu_ref, wo_ref))
    for w_tile, out_ref in zip((wg, wu, wo), w_out_refs):
        out_ref[...] = w_tile
    xn = xn_ref[...]
    h = (_silu(_dot(xn, wg)) * _dot(xn, wu)).astype(BF16)
    o_ref[...] += _dot(h, wo)

    @pl.when(f == pl.num_programs(1) - 1)
    def _():
        y = x_ref[...] + 0.5 * _rms(o_ref[...], post_ref[...])
        o_ref[...] = y
        if has_next:
            xn_ref[...] = _rms(y, refs[6][...]).astype(BF16)


def ffn_half(x, pre_w, post_w, w_in, w_out, which, *, tm, tf, next_norm_w=None):
    m, d = x.shape
    emit = which is not None
    has_next = next_norm_w is not None
    row_vec = pl.BlockSpec((1, d), lambda i, f: (0, 0))
    row_blk = pl.BlockSpec((tm, d), lambda i, f: (i, 0))
    if emit:
        assert m == tm
        lay, half = which
        dff = w_out.shape[2]
        nf = dff // tf
        w_specs = [
            pl.BlockSpec((None, None, d, tf), lambda i, f: (lay, half, 0, f)),
            pl.BlockSpec((None, None, d, tf), lambda i, f: (lay, half, 0, f + nf)),
            pl.BlockSpec((None, None, tf, d), lambda i, f: (lay, half, f, 0)),
        ]
        w_args = (w_in, w_in, w_out)
    else:
        dff = w_out.shape[0]
        w_specs = [
            pl.BlockSpec((d, tf), lambda i, f: (0, f)),
            pl.BlockSpec((d, tf), lambda i, f: (0, f)),
            pl.BlockSpec((tf, d), lambda i, f: (f, 0)),
        ]
        w_args = (*w_in, w_out)
    in_specs = [pl.BlockSpec((tm, d), lambda i, f: (i, 0), pipeline_mode=pl.Buffered(1)), row_vec, row_vec, *w_specs]
    args = [x, pre_w.reshape(1, d), post_w.reshape(1, d), *w_args]
    out_specs = [row_blk]
    out_shape = [jax.ShapeDtypeStruct((m, d), F32)]
    scratch = []
    if has_next:
        in_specs.append(row_vec)
        args.append(next_norm_w.reshape(1, d))
    if emit:
        out_specs += [
            pl.BlockSpec((d, tf), lambda i, f: (0, f)),
            pl.BlockSpec((d, tf), lambda i, f: (0, f)),
            pl.BlockSpec((tf, d), lambda i, f: (f, 0)),
        ]
        out_shape += [jax.ShapeDtypeStruct((d, dff), BF16)] * 2 + [jax.ShapeDtypeStruct((dff, d), BF16)]
    if has_next:
        out_specs.append(row_blk)
        out_shape.append(jax.ShapeDtypeStruct((m, d), BF16))
    else:
        scratch.append(pltpu.VMEM((tm, d), BF16))
    outs = pl.pallas_call(
        functools.partial(_ffn_kernel, emit_bf16=emit, has_next=has_next),
        grid=(m // tm, dff // tf),
        in_specs=in_specs,
        out_specs=out_specs,
        out_shape=out_shape,
        scratch_shapes=scratch,
        compiler_params=_cparams("parallel", "arbitrary"),
        name="ffn_half",
    )(*args)
    result = [outs[0]]
    if emit:
        result.append(((outs[1], outs[2]), outs[3]))
    if has_next:
        result.append(outs[-1])
    return result[0] if len(result) == 1 else tuple(result)


def _norm_matmul_kernel(*refs, has_norm, has_bias, n_out, emit_bf16, out_scale):
    x_ref, nw_ref, w_ref = refs[:3]
    b_ref = refs[3] if has_bias else None
    o_refs = refs[3 + has_bias:3 + has_bias + n_out]

    if has_norm:
        xn_ref = refs[-1]

        @pl.when(pl.program_id(1) == 0)
        def _():
            xn_ref[...] = _rms(x_ref[...], nw_ref[...]).astype(BF16)
    else:
        xn_ref = x_ref

    w_tile = w_ref[...].astype(BF16)
    if emit_bf16:
        refs[3 + has_bias + n_out][...] = w_tile
    acc = _dot(xn_ref[...], w_tile)
    if out_scale is not None:
        acc = acc * out_scale
    if has_bias:
        acc = -_softplus(-(acc + b_ref[...]))
    for o_ref in o_refs:
        o_ref[...] = acc.astype(o_ref.dtype)


def norm_matmul(x, norm_w, w, *, tm, tn, cols=None, out_dtypes=(F32,), logsig_bias=None, emit_bf16=False,
                out_scale=None):
    m, d = x.shape
    col0, n = cols if cols is not None else (0, w.shape[1])
    tn = min(tn, n)
    first = col0 // tn
    assert first * tn == col0 and n % tn == 0 and m % tm == 0 and (m == tm or not emit_bf16)
    has_bias = logsig_bias is not None
    has_norm = norm_w is not None
    if not has_norm:
        assert x.dtype == BF16
        norm_w = jnp.ones((d,), F32)
    in_specs = [
        pl.BlockSpec((tm, d), lambda i, j: (i, 0)),
        pl.BlockSpec((1, d), lambda i, j: (0, 0)),
        pl.BlockSpec((d, tn), lambda i, j: (0, first + j)),
    ]
    args = [x, norm_w.reshape(1, d), w]
    if has_bias:
        in_specs.append(pl.BlockSpec((1, tn), lambda i, j: (0, j)))
        args.append(logsig_bias.reshape(1, n))
    out_specs = [pl.BlockSpec((tm, tn), lambda i, j: (i, j)) for _ in out_dtypes]
    out_shape = [jax.ShapeDtypeStruct((m, n), dt) for dt in out_dtypes]
    if emit_bf16:
        out_specs.append(pl.BlockSpec((d, tn), lambda i, j: (0, j)))
        out_shape.append(jax.ShapeDtypeStruct((d, n), BF16))
    outs = pl.pallas_call(
        functools.partial(_norm_matmul_kernel, has_norm=has_norm, has_bias=has_bias, n_out=len(out_dtypes),
                          emit_bf16=emit_bf16, out_scale=out_scale),
        grid=(m // tm, n // tn),
        in_specs=in_specs,
        out_specs=out_specs,
        out_shape=out_shape,
        scratch_shapes=[pltpu.VMEM((tm, d), BF16)] if has_norm else [],
        compiler_params=_cparams("parallel", "arbitrary"),
        name="norm_matmul",
    )(*args)
    return outs[0] if len(outs) == 1 else outs


def _matmul_norm_res_kernel(a_ref, w_ref, post_ref, res_ref, o_ref):
    k = pl.program_id(1)

    @pl.when(k == 0)
    def _():
        o_ref[...] = jnp.zeros_like(o_ref)

    o_ref[...] += _dot(a_ref[...], w_ref[...])

    @pl.when(k == pl.num_programs(1) - 1)
    def _():
        o_ref[...] = res_ref[...] + _rms(o_ref[...], post_ref[...])


def matmul_norm_residual(a, w, post_w, res, *, tm, tk):
    m, kdim = a.shape
    n = w.shape[1]
    return pl.pallas_call(
        _matmul_norm_res_kernel,
        grid=(m // tm, kdim // tk),
        in_specs=[
            pl.BlockSpec((tm, tk), lambda i, k: (i, k)),
            pl.BlockSpec((tk, n), lambda i, k: (k, 0)),
            pl.BlockSpec((1, n), lambda i, k: (0, 0)),
            pl.BlockSpec((tm, n), lambda i, k: (i, 0)),
        ],
        out_specs=pl.BlockSpec((tm, n), lambda i, k: (i, 0)),
        out_shape=jax.ShapeDtypeStruct((m, n), F32),
        compiler_params=_cparams("parallel", "arbitrary"),
        name="matmul_norm_residual",
    )(a, w, post_w.reshape(1, n), res)


SUBLANES = 8
MXU_COLS = 256


def _proj_conv_kernel(x_ref, w_ref, cw_ref, cb_ref, act_ref, tail_ref, carry_ref, *, blocks_per_seq):
    i, j = pl.program_id(0), pl.program_id(1)
    tm, tn = act_ref.shape

    @pl.when(i % blocks_per_seq == 0)
    def _():
        carry_ref[j] = jnp.zeros((SUBLANES, tn), F32)

    x = x_ref[...]
    head_row = _iota((SUBLANES, MXU_COLS), 0)
    for c0 in range(0, tn, MXU_COLS):
        cols = slice(c0, c0 + MXU_COLS)
        raw = _dot(x, w_ref[:, cols])
        prev = carry_ref[j, :, cols]
        acc = cb_ref[:, cols]
        for tap in range(CONV_W):
            back = CONV_W - 1 - tap
            if back == 0:
                xs = raw
            else:
                rolled = pltpu.roll(raw, back, 0)
                head = jnp.where(head_row < back, pltpu.roll(prev, back, 0), rolled[:SUBLANES])
                xs = jnp.concatenate([head, rolled[SUBLANES:]], axis=0)
            acc = acc + xs * cw_ref[tap:tap + 1, cols]
        act_ref[:, cols] = _silu(acc)
        tail_ref[0, :, cols] = raw[tm - (CONV_W - 1):, :]
        carry_ref[j, :, cols] = raw[tm - SUBLANES:, :]


def proj_conv(xn, w, conv_w, conv_b, *, batch, tm, tn):
    t, d = xn.shape
    c = w.shape[1]
    blocks_per_seq = t // batch // tm
    assert blocks_per_seq * tm * batch == t and c % tn == 0
    act, tails = pl.pallas_call(
        functools.partial(_proj_conv_kernel, blocks_per_seq=blocks_per_seq),
        grid=(t // tm, c // tn),
        in_specs=[
            pl.BlockSpec((tm, d), lambda i, j: (i, 0)),
            pl.BlockSpec((d, tn), lambda i, j: (0, j)),
            pl.BlockSpec((CONV_W, tn), lambda i, j: (0, j)),
            pl.BlockSpec((1, tn), lambda i, j: (0, j)),
        ],
        out_specs=[
            pl.BlockSpec((tm, tn), lambda i, j: (i, j)),
            pl.BlockSpec((1, CONV_W - 1, tn), lambda i, j: (i, 0, j)),
        ],
        out_shape=[
            jax.ShapeDtypeStruct((t, c), F32),
            jax.ShapeDtypeStruct((t // tm, CONV_W - 1, c), F32),
        ],
        scratch_shapes=[pltpu.VMEM((c // tn, SUBLANES, tn), F32)],
        compiler_params=_cparams("arbitrary", "arbitrary"),
        name="proj_conv",
    )(xn, w, conv_w, conv_b.reshape(1, c))
    return act, tails[blocks_per_seq - 1::blocks_per_seq]


def _ssd_prep_kernel(dtr_ref, bias_ref, alog_ref, dt_ref, acs_ref):
    dt = _softplus(dtr_ref[...] + bias_ref[...])
    da = dt * (-jnp.exp(alog_ref[...]))
    lower = (_iota((CHUNK, CHUNK), 1) <= _iota((CHUNK, CHUNK), 0)).astype(F32)
    dt_ref[...] = dt
    acs_ref[...] = _dot_exact(lower, da)


def ssd_prep(dt_raw, dt_bias, a_log):
    t, nh = dt_raw.shape
    blk = pl.BlockSpec((CHUNK, nh), lambda c: (c, 0))
    vec = pl.BlockSpec((1, nh), lambda c: (0, 0))
    return pl.pallas_call(
        _ssd_prep_kernel,
        grid=(t // CHUNK,),
        in_specs=[blk, vec, vec],
        out_specs=[blk, blk],
        out_shape=[jax.ShapeDtypeStruct((t, nh), F32)] * 2,
        compiler_params=_cparams("parallel"),
        name="ssd_prep",
    )(dt_raw, dt_bias.reshape(1, nh), a_log.reshape(1, nh))


def _ssd_kernel(x_ref, b_ref, c_ref, z_ref, acsc_ref, acst_ref, dtt_ref, d_ref, gw_ref,
                y_ref, hfin_ref, ht_ref, yd_ref, *, heads_per_group):
    c = pl.program_id(2)
    p_n = SSM_HEAD_DIM
    pair = LANES // p_n
    assert CHUNK == D_STATE == LANES and heads_per_group % pair == 0

    @pl.when(c == 0)
    def _():
        ht_ref[...] = jnp.zeros_like(ht_ref)

    bm = b_ref[...]
    cm = c_ref[...]
    bm_t = bm.T
    cb = lax.dot_general(cm.astype(BF16), bm.astype(BF16), NT_DIMS, preferred_element_type=F32)
    causal = _iota((CHUNK, CHUNK), 1) <= _iota((CHUNK, CHUNK), 0)
    lane_head = _iota((CHUNK, LANES), 1) // p_n
    acsc = acsc_ref[0, 0]
    acst = acst_ref[0, 0]
    dtt = dtt_ref[0, 0]

    for slab in range(heads_per_group // pair):
        lanes = slice(slab * LANES, (slab + 1) * LANES)
        x_bf = x_ref[:, lanes].astype(BF16)
        ht = ht_ref[:, lanes]
        rhs = jnp.concatenate([x_bf, ht.astype(BF16)], axis=0)
        y_slab = s_slab = keep = None
        for k in range(pair):
            r = slab * pair + k
            a_col = jnp.broadcast_to(acsc[:, r:r + 1], (CHUNK, CHUNK))
            a_row = acst[r:r + 1, :]
            dt_row = dtt[r:r + 1, :]
            decay = jnp.exp(jnp.where(causal, a_col - a_row, -jnp.inf))
            within = cb * decay * dt_row
            carried = cm * jnp.exp(a_col)
            lhs = jnp.concatenate([within, carried], axis=1).astype(BF16)
            y_r = _dot(lhs, rhs)
            a_last = a_row[:, CHUNK - 1:CHUNK]
            to_end = jnp.exp(a_last - a_row) * dt_row
            s_r = _dot((bm_t * to_end).astype(BF16), x_bf)
            k_r = jnp.broadcast_to(jnp.exp(a_last), (D_STATE, LANES))
            if k == 0:
                y_slab, s_slab, keep = y_r, s_r, k_r
            else:
                mine = lane_head == k
                y_slab = jnp.where(mine, y_r, y_slab)
                s_slab = jnp.where(mine, s_r, s_slab)
                keep = jnp.where(mine, k_r, keep)
        yd_ref[:, lanes] = y_slab
        ht_ref[:, lanes] = ht * keep + s_slab

    y = yd_ref[...] + d_ref[...] * x_ref[...]
    y = y * _silu(z_ref[...])
    y_ref[...] = _rms(y, gw_ref[...]).astype(y_ref.dtype)

    @pl.when(c == pl.num_programs(2) - 1)
    def _():
        hfin_ref[0] = ht_ref[...].T


def ssd_prompt(xbc_act, z, dt, acs, d_exp, gnorm_w, *, batch):
    t, conv_dim = xbc_act.shape
    d_inner = z.shape[1]
    nh = dt.shape[1]
    groups = SSM_GROUPS
    r_n = nh // groups
    gp = d_inner // groups
    nc = t // batch // CHUNK
    n_chunks = t // CHUNK
    per_group = lambda a: a.reshape(n_chunks, CHUNK, groups, r_n).transpose(0, 2, 1, 3)
    acs_g = per_group(acs)
    acs_gt = acs_g.transpose(0, 1, 3, 2)
    dt_gt = per_group(dt).transpose(0, 1, 3, 2)
    xoff = d_inner // D_STATE
    row = lambda b, g, c: b * nc + c
    col_form = pl.BlockSpec((1, 1, CHUNK, r_n), lambda b, g, c: (row(b, g, c), g, 0, 0))
    row_form = pl.BlockSpec((1, 1, r_n, CHUNK), lambda b, g, c: (row(b, g, c), g, 0, 0))
    y, hfin = pl.pallas_call(
        functools.partial(_ssd_kernel, heads_per_group=r_n),
        grid=(batch, groups, nc),
        in_specs=[
            pl.BlockSpec((CHUNK, gp), lambda b, g, c: (row(b, g, c), g)),
            pl.BlockSpec((CHUNK, D_STATE), lambda b, g, c: (row(b, g, c), xoff + g)),
            pl.BlockSpec((CHUNK, D_STATE), lambda b, g, c: (row(b, g, c), xoff + groups + g)),
            pl.BlockSpec((CHUNK, gp), lambda b, g, c: (row(b, g, c), g)),
            col_form, row_form, row_form,
            pl.BlockSpec((1, gp), lambda b, g, c: (0, g)),
            pl.BlockSpec((1, gp), lambda b, g, c: (0, g)),
        ],
        out_specs=[
            pl.BlockSpec((CHUNK, gp), lambda b, g, c: (row(b, g, c), g)),
            pl.BlockSpec((1, gp, D_STATE), lambda b, g, c: (b, g, 0)),
        ],
        out_shape=[
            jax.ShapeDtypeStruct((t, d_inner), BF16),
            jax.ShapeDtypeStruct((batch, d_inner, D_STATE), F32),
        ],
        scratch_shapes=[pltpu.VMEM((D_STATE, gp), F32), pltpu.VMEM((CHUNK, gp), F32)],
        compiler_params=_cparams("parallel", "parallel", "arbitrary"),
        name="ssd_prompt",
    )(xbc_act, xbc_act, xbc_act, z, acs_g, acs_gt, dt_gt, d_exp.reshape(1, d_inner),
      gnorm_w.reshape(1, d_inner))
    return y, hfin


def _conv_step_kernel(st_ref, new_ref, w_ref, b_ref, act_ref, nst_ref):
    xn = new_ref[...]
    acc = b_ref[...]
    for tap in range(CONV_W - 1):
        acc = acc + st_ref[tap] * w_ref[tap:tap + 1, :]
        if tap > 0:
            nst_ref[tap - 1] = st_ref[tap]
    acc = acc + xn * w_ref[CONV_W - 1:CONV_W, :]
    nst_ref[CONV_W - 2] = xn
    act_ref[...] = _silu(acc)


def conv_step(state_t, xbc_new, conv_w, conv_b, *, tc):
    taps, batch, c = state_t.shape
    return pl.pallas_call(
        _conv_step_kernel,
        grid=(c // tc,),
        in_specs=[
            pl.BlockSpec((taps, batch, tc), lambda j: (0, 0, j)),
            pl.BlockSpec((batch, tc), lambda j: (0, j)),
            pl.BlockSpec((CONV_W, tc), lambda j: (0, j)),
            pl.BlockSpec((1, tc), lambda j: (0, j)),
        ],
        out_specs=[
            pl.BlockSpec((batch, tc), lambda j: (0, j)),
            pl.BlockSpec((taps, batch, tc), lambda j: (0, 0, j)),
        ],
        out_shape=[
            jax.ShapeDtypeStruct((batch, c), F32),
            jax.ShapeDtypeStruct((taps, batch, c), F32),
        ],
        compiler_params=_cparams("parallel"),
        name="conv_step",
    )(state_t, xbc_new, conv_w, conv_b.reshape(1, c))


def _ssd_step_kernel(h_ref, xs_ref, b_ref, c_ref, z_ref, dtr_ref, bias_ref, alog_ref, d_ref, gw_ref,
                     ho_ref, y_ref, dte_ref, dece_ref, *, heads_per_group):
    g = pl.program_id(0)
    r_n, p_n = heads_per_group, SSM_HEAD_DIM
    gp = r_n * p_n
    batch, nh = dtr_ref.shape
    rows_per_tile = LANES // p_n

    expand = (_iota((nh, gp), 1) // p_n + g * r_n == _iota((nh, gp), 0)).astype(F32)
    dt = _softplus(dtr_ref[...] + bias_ref[...])
    dte_ref[...] = _dot_exact(dt, expand)
    dece_ref[...] = _dot_exact(jnp.exp(dt * (-jnp.exp(alog_ref[...]))), expand)

    def body(b, carry):
        x = xs_ref[pl.ds(b, 1), :]
        bv = b_ref[pl.ds(b, 1), :]
        cv = c_ref[pl.ds(b, 1), :]
        xdt = x * dte_ref[pl.ds(b, 1), :]
        dec = dece_ref[pl.ds(b, 1), :]
        y_parts = []
        for j in range(gp // LANES):
            lanes = slice(j * LANES, (j + 1) * LANES)
            heads = pl.ds(j * rows_per_tile, rows_per_tile)
            h = h_ref[b, heads].reshape(LANES, D_STATE)
            h_new = h * _col_bcast(dec[:, lanes]) + _col_bcast(xdt[:, lanes]) * bv
            ho_ref[b, heads] = h_new.reshape(rows_per_tile, p_n, D_STATE)
            ycol = jnp.sum(h_new * cv, axis=-1, keepdims=True)
            y_parts.append(jnp.broadcast_to(ycol, (LANES, LANES)).T[0:1, :])
        y = jnp.concatenate(y_parts, axis=1) + d_ref[...] * x
        y = y * _silu(z_ref[pl.ds(b, 1), :])
        y_ref[pl.ds(b, 1), :] = _rms(y, gw_ref[...])
        return carry

    lax.fori_loop(0, batch, body, 0)


def ssd_step(h, xbc_act, z, dt_raw, dt_bias, a_log, d_exp, gnorm_w):
    batch, nh, p_n, n_n = h.shape
    d_inner = nh * p_n
    groups = SSM_GROUPS
    r_n = nh // groups
    gp = d_inner // groups
    xoff = d_inner // D_STATE
    return pl.pallas_call(
        functools.partial(_ssd_step_kernel, heads_per_group=r_n),
        grid=(groups,),
        in_specs=[
            pl.BlockSpec((batch, r_n, p_n, n_n), lambda g: (0, g, 0, 0)),
            pl.BlockSpec((batch, gp), lambda g: (0, g)),
            pl.BlockSpec((batch, D_STATE), lambda g: (0, xoff + g)),
            pl.BlockSpec((batch, D_STATE), lambda g: (0, xoff + groups + g)),
            pl.BlockSpec((batch, gp), lambda g: (0, g)),
            pl.BlockSpec((batch, nh), lambda g: (0, 0)),
            pl.BlockSpec((1, nh), lambda g: (0, 0)),
            pl.BlockSpec((1, nh), lambda g: (0, 0)),
            pl.BlockSpec((1, gp), lambda g: (0, g)),
            pl.BlockSpec((1, gp), lambda g: (0, g)),
        ],
        out_specs=[
            pl.BlockSpec((batch, r_n, p_n, n_n), lambda g: (0, g, 0, 0)),
            pl.BlockSpec((batch, gp), lambda g: (0, g)),
        ],
        out_shape=[
            jax.ShapeDtypeStruct(h.shape, F32),
            jax.ShapeDtypeStruct((batch, d_inner), F32),
        ],
        scratch_shapes=[pltpu.VMEM((batch, gp), F32), pltpu.VMEM((batch, gp), F32)],
        compiler_params=_cparams("parallel"),
        name="ssd_step",
    )(h, xbc_act, xbc_act, xbc_act, z, dt_raw, dt_bias.reshape(1, nh), a_log.reshape(1, nh),
      d_exp.reshape(1, d_inner), gnorm_w.reshape(1, d_inner))


def _cumsum_kernel(x_ref, o_ref, *, tile):
    seq = x_ref.shape[1]
    lower = (_iota((tile, tile), 1) <= _iota((tile, tile), 0)).astype(F32)
    carry = jnp.zeros((1, x_ref.shape[2]), F32)
    for i in range(seq // tile):
        cs = _dot_exact(lower, x_ref[0, i * tile:(i + 1) * tile, :]) + carry
        o_ref[0, i * tile:(i + 1) * tile, :] = cs
        carry = cs[tile - 1:tile, :]


def cumsum_seq(x3, *, tile=256):
    batch, seq, n = x3.shape
    blk = pl.BlockSpec((1, seq, n), lambda b: (b, 0, 0))
    return pl.pallas_call(
        functools.partial(_cumsum_kernel, tile=tile),
        grid=(batch,),
        in_specs=[blk],
        out_specs=blk,
        out_shape=jax.ShapeDtypeStruct(x3.shape, F32),
        compiler_params=_cparams("parallel"),
        name="cumsum_seq",
    )(x3)


def _fox_prompt_kernel(q_ref, k_ref, v_ref, c_ref, o_ref, *, tq):
    q = q_ref[...]
    sub = tq // LANES
    n_q = k_ref.shape[0] // tq

    def block(ki, carry, cq, masked):
        m, l, acc = carry
        k = k_ref[ki * tq:(ki + 1) * tq, :]
        v = v_ref[ki * tq:(ki + 1) * tq, :]
        ck_rows = c_ref[0, ki * sub:(ki + 1) * sub, :] * LOG2E
        ck = jnp.concatenate([jnp.broadcast_to(ck_rows[j:j + 1, :], (tq, LANES)) for j in range(sub)], axis=1)
        s = lax.dot_general(q, k, NT_DIMS, preferred_element_type=F32)
        s = s + cq - ck
        if masked:
            s = jnp.where(_iota((tq, tq), 1) <= _iota((tq, tq), 0), s, -jnp.inf)
        m_new = jnp.maximum(m, jnp.max(s, axis=-1, keepdims=True))
        alpha = jnp.exp2(m - m_new)
        p = jnp.exp2(s - m_new)
        l = alpha * l + jnp.sum(p, axis=-1, keepdims=True)
        acc = alpha * acc + _dot(p.astype(BF16), v)
        return m_new, l, acc

    for n_before in range(n_q):
        @pl.when(pl.program_id(2) == n_before)
        def _(n_before=n_before):
            cq_rows = c_ref[0, n_before * sub:(n_before + 1) * sub, :] * LOG2E
            cq = jnp.concatenate([_col_bcast(cq_rows[i:i + 1, :]) for i in range(sub)], axis=0)
            cq = jnp.concatenate([cq] * sub, axis=1)
            carry = (jnp.full((tq, 1), -jnp.inf, F32), jnp.zeros((tq, 1), F32), jnp.zeros((tq, q.shape[1]), F32))
            for ki in range(n_before):
                carry = block(ki, carry, cq, False)
            _, l, acc = block(n_before, carry, cq, True)
            o_ref[...] = (acc / l).astype(o_ref.dtype)


def fox_prompt_q_scale(dh):
    return LOG2E / (dh ** 0.5)


def fox_prompt(q, k, v, c_tab, *, batch, heads, tq):
    t, hd = q.shape
    dh = hd // heads
    seq = t // batch
    nq = seq // tq
    return pl.pallas_call(
        functools.partial(_fox_prompt_kernel, tq=tq),
        grid=(batch, heads, nq),
        in_specs=[
            pl.BlockSpec((tq, dh), lambda b, h, i: (b * nq + i, h)),
            pl.BlockSpec((seq, dh), lambda b, h, i: (b, h)),
            pl.BlockSpec((seq, dh), lambda b, h, i: (b, h)),
            pl.BlockSpec((1, seq // LANES, LANES), lambda b, h, i: (b * heads + h, 0, 0)),
        ],
        out_specs=pl.BlockSpec((tq, dh), lambda b, h, i: (b * nq + i, h)),
        out_shape=jax.ShapeDtypeStruct((t, hd), BF16),
        compiler_params=_cparams("parallel", "parallel", "arbitrary"),
        name="fox_prompt",
    )(q, k, v, c_tab)


BIAS_PAGES_PER_STEP = 16
ATTN_PAGES_PER_STEP = 4


def _visit_page(pt, batch_row, visit, n_pages):
    return pt[batch_row * n_pages + (n_pages - 1 - visit)]


def _forget_bias_kernel(*refs, n_sub):
    lfn_ref = refs[1]
    lf_refs = refs[2:2 + n_sub]
    o_ref, run_ref = refs[2 + n_sub:]
    page = lf_refs[0].shape[1]

    @pl.when(pl.program_id(1) == 0)
    def _():
        run_ref[...] = lfn_ref[0]

    later = (_iota((page, page), 1) > _iota((page, page), 0)).astype(F32)
    run = run_ref[...]
    for j in range(n_sub):
        lf = lf_refs[j][0]
        o_ref[0, j] = _dot_exact(later, lf) + run
        run = run + jnp.sum(lf, axis=0, keepdims=True)
    run_ref[...] = run


def forget_bias(logf_new, cache_logf, page_table):
    batch, heads = logf_new.shape
    n_phys, page, _ = cache_logf.shape
    n_pages = page_table.shape[1]
    n_sub = BIAS_PAGES_PER_STEP

    def page_spec(j):
        return pl.BlockSpec((1, page, heads), lambda b, s, pt: (_visit_page(pt, b, s * n_sub + j, n_pages), 0, 0))

    grid_spec = pltpu.PrefetchScalarGridSpec(
        num_scalar_prefetch=1,
        grid=(batch, n_pages // n_sub),
        in_specs=[pl.BlockSpec((1, 1, heads), lambda b, s, pt: (b, 0, 0))] + [page_spec(j) for j in range(n_sub)],
        out_specs=pl.BlockSpec((1, n_sub, page, heads), lambda b, s, pt: (b, s, 0, 0)),
        scratch_shapes=[pltpu.VMEM((1, heads), F32)],
    )
    return pl.pallas_call(
        functools.partial(_forget_bias_kernel, n_sub=n_sub),
        grid_spec=grid_spec,
        out_shape=jax.ShapeDtypeStruct((batch, n_pages, page, heads), F32),
        compiler_params=_cparams("parallel", "arbitrary"),
        name="forget_bias",
    )(page_table.reshape(-1), logf_new.reshape(batch, 1, heads), *([cache_logf] * n_sub))


def _fox_sample_kernel(*refs, n_sub, scale):
    q_ref, kn_ref, vn_ref = refs[1:4]
    page_refs = refs[4:4 + 3 * n_sub]
    o_ref, m_ref, l_ref, acc_ref = refs[4 + 3 * n_sub:]
    step = pl.program_id(1)
    heads, dh = q_ref.shape[1:]
    rows = page_refs[0].shape[1] * heads
    own_head = _iota((heads, rows), 1) % heads == _iota((heads, rows), 0)

    @pl.when(step == 0)
    def _():
        m_ref[...] = jnp.full_like(m_ref, -jnp.inf)
        l_ref[...] = jnp.zeros_like(l_ref)
        acc_ref[...] = jnp.zeros_like(acc_ref)

    q = q_ref[0].astype(BF16)
    m, l, acc = m_ref[...], l_ref[...], acc_ref[...]
    for j in range(n_sub):
        ck_ref, cv_ref, bias_ref = page_refs[3 * j:3 * j + 3]
        k = ck_ref[0].reshape(rows, dh).astype(BF16)
        v = cv_ref[0].reshape(rows, dh).astype(BF16)
        s = lax.dot_general(q, k, NT_DIMS, preferred_element_type=F32) * scale + bias_ref[0, 0]
        s = jnp.where(own_head, s, -jnp.inf)
        m_new = jnp.maximum(m, jnp.max(s, axis=-1, keepdims=True))
        alpha = jnp.exp(m - m_new)
        pr = jnp.exp(s - m_new)
        l = alpha * l + jnp.sum(pr, axis=-1, keepdims=True)
        acc = alpha * acc + _dot(pr.astype(BF16), v)
        m = m_new
    m_ref[...], l_ref[...], acc_ref[...] = m, l, acc

    @pl.when(step == pl.num_programs(1) - 1)
    def _():
        qf = q.astype(F32)
        kn = kn_ref[0].astype(BF16).astype(F32)
        vn = vn_ref[0].astype(BF16).astype(F32)
        s_new = jnp.sum(qf * kn, axis=-1, keepdims=True) * scale
        m_fin = jnp.maximum(m, s_new)
        a_fin = jnp.exp(m - m_fin)
        p_new = jnp.exp(s_new - m_fin)
        l_fin = a_fin * l + p_new
        o_ref[0] = (a_fin * acc + p_new.astype(BF16).astype(F32) * vn) / l_fin


def fox_sample(q, k_new, v_new, logf_new, cache_k, cache_v, cache_logf, page_table):
    batch, heads, dh = q.shape
    page = cache_k.shape[1]
    n_pages = page_table.shape[1]
    n_sub = ATTN_PAGES_PER_STEP
    bias = forget_bias(logf_new, cache_logf, page_table)
    bias = bias.reshape(batch, n_pages, 1, page * heads)

    new_spec = pl.BlockSpec((1, heads, dh), lambda b, s, pt: (b, 0, 0))
    in_specs = [new_spec, new_spec, new_spec]
    operands = [q, k_new, v_new]
    for j in range(n_sub):
        phys = lambda b, s, pt, j=j: (_visit_page(pt, b, s * n_sub + j, n_pages), 0, 0, 0)
        in_specs += [
            pl.BlockSpec((1, page, heads, dh), phys),
            pl.BlockSpec((1, page, heads, dh), phys),
            pl.BlockSpec((1, 1, 1, page * heads), lambda b, s, pt, j=j: (b, s * n_sub + j, 0, 0)),
        ]
        operands += [cache_k, cache_v, bias]
    grid_spec = pltpu.PrefetchScalarGridSpec(
        num_scalar_prefetch=1,
        grid=(batch, n_pages // n_sub),
        in_specs=in_specs,
        out_specs=pl.BlockSpec((1, heads, dh), lambda b, s, pt: (b, 0, 0)),
        scratch_shapes=[pltpu.VMEM((heads, 1), F32), pltpu.VMEM((heads, 1), F32), pltpu.VMEM((heads, dh), F32)],
    )
    return pl.pallas_call(
        functools.partial(_fox_sample_kernel, n_sub=n_sub, scale=1.0 / (dh ** 0.5)),
        grid_spec=grid_spec,
        out_shape=jax.ShapeDtypeStruct((batch, heads, dh), F32),
        compiler_params=_cparams("parallel", "arbitrary"),
        name="fox_sample",
    )(page_table.reshape(-1), *operands)


def _tiles(m):
    return min(m, ROW_TILE)


def _trunk(y, w, *, mamba_mixer, fox_mixer, q_scale=None, xbc_proj=None):
    m, d = y.shape
    tm = _tiles(m)
    nw = w["norm_w"]
    extras = {}

    def ffn(y, layer, half, next_norm_w=None):
        pre, post = nw[layer, 4 * half], nw[layer, 4 * half + 1]
        kw = dict(tm=tm, tf=FFN_TF, next_norm_w=next_norm_w)
        if (layer, half) in w["ffn_bf16"]:
            w_in, w_out = w["ffn_bf16"][layer, half]
            return ffn_half(y, pre, post, w_in, w_out, None, **kw)
        y, w["ffn_bf16"][layer, half], *rest = ffn_half(y, pre, post, w["ffn_in"], w["ffn_out"], (layer, half), **kw)
        return (y, *rest) if rest else y

    def proj(xn, name, cols, out_dtypes=(F32,), out_scale=None):
        key = (name, cols)
        kw = dict(tm=tm, tn=PROJ_TN, out_dtypes=out_dtypes, out_scale=out_scale)
        if key in w["proj_bf16"]:
            return norm_matmul(xn, None, w["proj_bf16"][key], **kw)
        *outs, w["proj_bf16"][key] = norm_matmul(xn, None, w[name], cols=cols, emit_bf16=True, **kw)
        return outs[0] if len(outs) == 1 else outs

    hd = w["w_q"].shape[1]
    d_inner, conv_dim, nh = w["in_split"]
    y, xn = ffn(y, 0, 0, next_norm_w=nw[0, 2])
    z = proj(xn, "in_proj", (0, d_inner))
    xbc = proj(xn, "in_proj", (d_inner, conv_dim)) if xbc_proj is None else xbc_proj(xn)
    dt_raw = proj(xn, "in_proj", (d_inner + conv_dim, nh))
    mixed, extras["conv"], extras["ssm"] = mamba_mixer(z, xbc, dt_raw)
    y = matmul_norm_residual(mixed, w["out_proj"], nw[0, 3], y, tm=tm, tk=PROJ_TK)
    y, xn = ffn(y, 0, 1, next_norm_w=w["kv_norm_w"])
    extras["k"], extras["k_bf"] = proj(xn, "w_kv", (0, hd), out_dtypes=(F32, BF16))
    extras["v"], extras["v_bf"] = proj(xn, "w_kv", (hd, hd), out_dtypes=(F32, BF16))
    extras["logf"] = norm_matmul(xn, None, w["w_f"], tm=tm, tn=LANES, logsig_bias=w["b_f"])
    y, xn = ffn(y, 1, 0, next_norm_w=nw[1, 2])
    q = proj(xn, "w_q", (0, hd), out_dtypes=(BF16,), out_scale=q_scale)
    attn = fox_mixer(q, extras)
    y = matmul_norm_residual(attn, w["w_o"], nw[1, 3], y, tm=tm, tk=PROJ_TK)
    y = ffn(y, 1, 1)
    return y, extras


def kernel(x_prompt, x_sample, state_conv, state_ssm, cache_k, cache_v, cache_logf, page_table,
           norm_w, ffn_w_in, ffn_w_out, m_in_proj, m_conv_w, m_conv_b, m_dt_bias, m_A_log, m_D,
           m_gnorm_w, m_out_proj, kv_norm_w, w_kvf, b_fg, w_q, w_o):
    bp, seq, d = x_prompt.shape
    bs = x_sample.shape[0]
    heads = cache_k.shape[2]
    dh = cache_k.shape[3]
    hd = heads * dh
    nh = m_dt_bias.shape[1]
    d_inner = nh * SSM_HEAD_DIM
    conv_dim = m_conv_w.shape[2]

    pad_f = LANES - heads
    w = {
        "norm_w": norm_w,
        "ffn_in": ffn_w_in,
        "ffn_out": ffn_w_out,
        "ffn_bf16": {},
        "in_proj": m_in_proj[0],
        "in_split": (d_inner, conv_dim, nh),
        "w_kv": w_kvf,
        "w_q": w_q[0],
        "proj_bf16": {},
        "out_proj": m_out_proj[0].astype(BF16),
        "w_o": w_o[0].astype(BF16),
        "kv_norm_w": kv_norm_w,
        "w_f": jnp.pad(w_kvf[:, 2 * hd:], ((0, 0), (0, pad_f))).astype(BF16),
        "b_f": jnp.pad(b_fg, (0, pad_f)),
    }
    d_exp = jnp.repeat(m_D[0], SSM_HEAD_DIM)
    conv_w, conv_b = m_conv_w[0], m_conv_b[0]

    def xbc_conv_prompt(xn):
        w_xbc = w["proj_bf16"]["in_proj", (d_inner, conv_dim)]
        return proj_conv(xn, w_xbc, conv_w, conv_b, batch=bp, tm=_tiles(bp * seq), tn=PROJ_TN)

    def mamba_prompt(z, xbc_act_tail, dt_raw):
        act, tail = xbc_act_tail
        dt, acs = ssd_prep(dt_raw, m_dt_bias[0], m_A_log[0])
        yb, hfin = ssd_prompt(act, z, dt, acs, d_exp, m_gnorm_w[0], batch=bp)
        return yb, tail, hfin.reshape(bp, nh, SSM_HEAD_DIM, D_STATE)

    def fox_prompt_mixer(q, ex):
        logf3 = ex["logf"].reshape(bp, seq, LANES)
        c = cumsum_seq(logf3)[:, :, :heads]
        c_tab = c.transpose(0, 2, 1).reshape(bp * heads, seq // LANES, LANES)
        return fox_prompt(q, ex["k_bf"], ex["v_bf"], c_tab, batch=bp, heads=heads, tq=FOX_TQ)

    def mamba_sample(z, xbc, dt_raw):
        st = state_conv[0].transpose(1, 0, 2)
        act, nst = conv_step(st, xbc, conv_w, conv_b, tc=CONV_STEP_TC)
        h_new, yg = ssd_step(state_ssm[0], act, z, dt_raw, m_dt_bias[0], m_A_log[0], d_exp, m_gnorm_w[0])
        yg = jnp.pad(yg, ((0, SAMPLE_ROWS - bs), (0, 0))).astype(BF16)
        return yg, nst.transpose(1, 0, 2), h_new

    def fox_sample_mixer(q, ex):
        per_head = lambda a: a[:bs].astype(F32).reshape(bs, heads, dh)
        attn = fox_sample(per_head(q), per_head(ex["k"]), per_head(ex["v"]), ex["logf"][:bs, :heads],
                          cache_k, cache_v, cache_logf, page_table)
        return jnp.pad(attn.reshape(bs, hd), ((0, SAMPLE_ROWS - bs), (0, 0))).astype(BF16)

    xs_rows = jnp.pad(x_sample.reshape(bs, d), ((0, SAMPLE_ROWS - bs), (0, 0)))
    y_s, ex_s = _trunk(xs_rows, w, mamba_mixer=mamba_sample, fox_mixer=fox_sample_mixer)
    y_p, ex_p = _trunk(x_prompt.reshape(bp * seq, d), w, mamba_mixer=mamba_prompt, fox_mixer=fox_prompt_mixer,
                       q_scale=fox_prompt_q_scale(dh), xbc_proj=xbc_conv_prompt)

    return (
        y_p.reshape(bp, seq, d),
        y_s[:bs].reshape(bs, 1, d),
        ex_p["conv"][None],
        ex_p["ssm"][None],
        ex_p["k"].reshape(bp, seq, heads, dh),
        ex_p["v"].reshape(bp, seq, heads, dh),
        ex_p["logf"][:, :heads].reshape(bp, seq, heads),
        ex_s["conv"][None],
        ex_s["ssm"][None],
        ex_s["k"][:bs].reshape(bs, 1, heads, dh),
        ex_s["v"][:bs].reshape(bs, 1, heads, dh),
        ex_s["logf"][:bs, :heads].reshape(bs, 1, heads),
    )
```
